```python
import math
import jax, jax.numpy as jnp
from jax import lax
import numpy as np

D_MODEL = 2048
BATCH = 4
SEQ = 2048
DEPTH = 4
DEC_BATCH = 128
DEC_SEQ = 8
PAST_LEN = 16384
PAGE_SIZE = 128

N_AB = (DEPTH + 1) // 2
N_C = DEPTH // 2
HEAD = 64
W_A = D_MODEL // 2
H_A = W_A // HEAD
LORA_W = 64
LORA_A = 64
LORA_G = 160
COLS_A = 3 * W_A + LORA_W + LORA_A + LORA_G
W_B = D_MODEL // 2
N_BLK_B = W_B // HEAD
CONV_W = 4
LRU_C = 8.0
COLS_AB = COLS_A + 2 * W_B
W_C = D_MODEL
GRP_C = 16
N_GRP_C = W_C // GRP_C
P_C = 64
D_FF = 5632
D_PLE = 256
RMS_EPS = 1e-6
GN_EPS = 64e-5

kernel_name = 'hybrid_rwkv7_rglru_s5_step'


def rmsnorm(x, g):
    xf = x.astype(jnp.float32)
    y = xf * lax.rsqrt(jnp.mean(xf * xf, axis=-1, keepdims=True) + RMS_EPS)
    return (y * g.astype(jnp.float32)).astype(x.dtype)


def swiglu(h, wg, wu, wd):
    return (jax.nn.silu(h @ wg) * (h @ wu)) @ wd


def _linear_combine(e1, e2):
    a1, b1 = e1
    a2, b2 = e2
    return (a1 * a2, a2 * b1 + b2)


def _complex_linear_combine(e1, e2):
    a1r, a1i, b1r, b1i = e1
    a2r, a2i, b2r, b2i = e2
    return (a2r * a1r - a2i * a1i, a2r * a1i + a2i * a1r,
            a2r * b1r - a2i * b1i + b2r, a2r * b1i + a2i * b1r + b2i)


def rwkv7_mix(za, shift_prev, wkv0, mu, w0, w2, a0, a2, g2, k_k, k_a, r_k, lnx_g, lnx_b):
    f32 = jnp.float32
    bsz, t = za.shape[0], za.shape[1]
    za = za.astype(f32)
    zprev = jnp.concatenate([shift_prev[:, None, :].astype(f32), za[:, :-1]], axis=1)
    zs = za + (zprev - za) * mu.astype(f32)
    r, k, v, xw, xa, xg = jnp.split(
        zs, [W_A, 2 * W_A, 3 * W_A, 3 * W_A + LORA_W, 3 * W_A + LORA_W + LORA_A], axis=-1)
    w_log = -jax.nn.softplus(-(w0 + jnp.tanh(xw) @ w2)) - 0.5
    decay = jnp.exp(-jnp.exp(w_log))
    a = jax.nn.sigmoid(a0 + xa @ a2)
    g = jax.nn.sigmoid(xg) @ g2
    kk = k * k_k
    k = k * (1.0 + (a - 1.0) * k_a)
    heads = lambda u: u.reshape(bsz, t, H_A, HEAD)
    kk = heads(kk)
    kk = kk / jnp.maximum(jnp.sqrt(jnp.sum(kk * kk, axis=-1, keepdims=True)), 1e-12)
    r, k, v, decay, a = heads(r), heads(k), heads(v), heads(decay), heads(a)

    def step(S, inp):
        r_t, w_t, k_t, v_t, kk_t, a_t = inp
        sa = jnp.einsum('bhij,bhj->bhi', S, -kk_t)
        S = (S * w_t[:, :, None, :] + sa[..., None] * (kk_t * a_t)[:, :, None, :]
             + v_t[..., None] * k_t[:, :, None, :])
        return S, jnp.einsum('bhij,bhj->bhi', S, r_t)

    tmaj = lambda u: jnp.swapaxes(u, 0, 1)
    s_last, y = lax.scan(step, wkv0.astype(f32), tuple(tmaj(u) for u in (r, decay, k, v, kk, a)))
    y = tmaj(y)
    mu_y = jnp.mean(y, axis=-1, keepdims=True)
    var = jnp.mean(jnp.square(y - mu_y), axis=-1, keepdims=True)
    yn = ((y - mu_y) * lax.rsqrt(var + GN_EPS)).reshape(bsz, t, W_A) * lnx_g + lnx_b
    bonus = (jnp.sum(r * k * r_k, axis=-1, keepdims=True) * v).reshape(bsz, t, W_A)
    return (yn + bonus) * g, s_last, za[:, -1]


def rglru_mix(xb, gb, h0, conv_buf, conv_w, conv_b, wa, ba, wx, bx, lam):
    f32 = jnp.float32
    bsz, t = xb.shape[0], xb.shape[1]
    xpad = jnp.concatenate([conv_buf.astype(f32), xb.astype(f32)], axis=1)
    xc = conv_b.astype(f32)
    for j in range(CONV_W):
        xc = xc + xpad[:, j:j + t] * conv_w[j]
    blk = xc.reshape(bsz, t, N_BLK_B, HEAD)
    gate_r = jax.nn.sigmoid(jnp.einsum('btnh,nhk->btnk', blk, wa).reshape(bsz, t, W_B) + ba)
    gate_i = jax.nn.sigmoid(jnp.einsum('btnh,nhk->btnk', blk, wx).reshape(bsz, t, W_B) + bx)
    log_a = -LRU_C * gate_r * jax.nn.softplus(-lam)
    a = jnp.exp(log_a)
    b = jnp.sqrt(-jnp.expm1(2.0 * log_a)) * (gate_i * xc)
    b = b.at[:, 0].add(a[:, 0] * h0.astype(f32))
    _, h = lax.associative_scan(_linear_combine, (a, b), axis=1)
    y = h * jax.nn.gelu(gb.astype(f32))
    return y, h[:, -1], xpad[:, xpad.shape[1] - (CONV_W - 1):]


def s5_mix(u, h0_re, h0_im, a_re, a_im, log_dt, b_re, b_im, c_re, c_im, d_skip, w_glu, b_glu):
    f32 = jnp.float32
    bsz, t = u.shape[0], u.shape[1]
    uf = u.astype(f32)
    ug = uf.reshape(bsz, t, N_GRP_C, GRP_C)
    a_re = a_re.astype(f32)
    a_im = a_im.astype(f32)
    dt = jnp.exp(log_dt.astype(f32))[:, None]
    mag = jnp.exp(dt * a_re)
    ab_re, ab_im = mag * jnp.cos(dt * a_im), mag * jnp.sin(dt * a_im)
    den = a_re * a_re + a_im * a_im
    f_re = ((ab_re - 1.0) * a_re + ab_im * a_im) / den
    f_im = (ab_im * a_re - (ab_re - 1.0) * a_im) / den
    bb_re = f_re[..., None] * b_re - f_im[..., None] * b_im
    bb_im = f_re[..., None] * b_im + f_im[..., None] * b_re
    bu_re = jnp.einsum('btgc,gpc->btgp', ug, bb_re)
    bu_im = jnp.einsum('btgc,gpc->btgp', ug, bb_im)
    h0_re = h0_re.astype(f32)
    h0_im = h0_im.astype(f32)
    bu_re = bu_re.at[:, 0].add(ab_re * h0_re - ab_im * h0_im)
    bu_im = bu_im.at[:, 0].add(ab_re * h0_im + ab_im * h0_re)
    shp = (1, t, N_GRP_C, P_C)
    _, _, h_re, h_im = lax.associative_scan(
        _complex_linear_combine,
        (jnp.broadcast_to(ab_re, shp), jnp.broadcast_to(ab_im, shp), bu_re, bu_im), axis=1)
    y = (jnp.einsum('btgp,gcp->btgc', h_re, c_re)
         - jnp.einsum('btgp,gcp->btgc', h_im, c_im)).reshape(bsz, t, W_C) + d_skip * uf
    z = jax.nn.gelu(y)
    return z * jax.nn.sigmoid(z @ w_glu + b_glu), h_re[:, -1], h_im[:, -1]


def trunk(x, p, st_wkv, st_shift, st_h, st_conv, st_cre, st_cim, w):
    new_wkv, new_shift, new_h, new_conv, new_cre, new_cim = [], [], [], [], [], []
    for l in range(DEPTH):
        j = l // 2
        x = x + 0.5 * swiglu(rmsnorm(x, w['norm_ffn1'][l]), w['ffn1_wg'][l], w['ffn1_wu'][l], w['ffn1_wd'][l])
        hmix = rmsnorm(x, w['norm_mix'][l])
        if l % 2 == 0:
            z = hmix @ w['w_in_ab'][j]
            za, zbx, zbg = jnp.split(z, [COLS_A, COLS_A + W_B], axis=-1)
            ya, s_wkv, s_shift = rwkv7_mix(
                za, st_shift[j], st_wkv[j], w['mu_a'][j], w['w0_a'][j], w['w2_a'][j], w['a0_a'][j],
                w['a2_a'][j], w['g2_a'][j], w['kk_a'][j], w['ka_a'][j], w['rk_a'][j],
                w['lnx_g'][j], w['lnx_b'][j])
            yb, s_h, s_conv = rglru_mix(
                zbx, zbg, st_h[j], st_conv[j], w['conv_w_b'][j], w['conv_b_b'][j], w['wa_b'][j],
                w['ba_b'][j], w['wx_b'][j], w['bx_b'][j], w['lam_b'][j])
            x = x + jnp.concatenate([ya, yb], axis=-1).astype(x.dtype) @ w['w_out_ab'][j]
            new_wkv.append(s_wkv)
            new_shift.append(s_shift)
            new_h.append(s_h)
            new_conv.append(s_conv)
        else:
            yc, s_re, s_im = s5_mix(
                hmix, st_cre[j], st_cim[j], w['a_re_c'][j], w['a_im_c'][j], w['log_dt_c'][j],
                w['b_re_c'][j], w['b_im_c'][j], w['c_re_c'][j], w['c_im_c'][j], w['d_c'][j],
                w['w_glu_c'][j], w['b_glu_c'][j])
            x = x + yc.astype(x.dtype)
            new_cre.append(s_re)
            new_cim.append(s_im)
        x = x + 0.5 * swiglu(rmsnorm(x, w['norm_ffn2'][l]), w['ffn2_wg'][l], w['ffn2_wu'][l], w['ffn2_wd'][l])
        gate = jax.nn.sigmoid(rmsnorm(x, w['norm_ple'][l]) @ w['ple_gate'][l])
        x = x + gate * (p[l] @ w['ple_proj'][l])
    y = rmsnorm(x, w['final_norm'])
    stk = lambda lst, ref: jnp.stack(lst).astype(ref.dtype)
    return y, (stk(new_wkv, st_wkv), stk(new_shift, st_shift), stk(new_h, st_h),
               stk(new_conv, st_conv), stk(new_cre, st_cre), stk(new_cim, st_cim))


def setup_inputs(seed: int = 0) -> dict:
    key = jax.random.key(seed)
    keys = jax.random.split(key, 96)
    counter = [0]
    f32 = jnp.float32

    def nk():
        k = keys[counter[0]]
        counter[0] += 1
        return k

    nrm = lambda shape, scale: jax.random.normal(nk(), shape, f32) * scale
    gain = lambda shape: 1.0 + 0.01 * jax.random.normal(nk(), shape, f32)
    unif = lambda shape, lo, hi: jax.random.uniform(nk(), shape, f32, lo, hi)

    d = {}
    d['x_prompt'] = nrm((BATCH, SEQ, D_MODEL), 1.0)
    d['x_sample'] = nrm((DEC_BATCH, DEC_SEQ, D_MODEL), 1.0)
    d['state_a_wkv'] = nrm((N_AB, DEC_BATCH, H_A, HEAD, HEAD), 0.5)
    d['state_a_shift'] = nrm((N_AB, DEC_BATCH, COLS_A), 1.0)
    d['state_b_h'] = nrm((N_AB, DEC_BATCH, W_B), 0.5)
    d['state_b_conv'] = nrm((N_AB, DEC_BATCH, CONV_W - 1, W_B), 1.0)
    d['state_c_re'] = nrm((N_C, DEC_BATCH, N_GRP_C, P_C), 0.1)
    d['state_c_im'] = nrm((N_C, DEC_BATCH, N_GRP_C, P_C), 0.1)
    d['p_prompt'] = nrm((DEPTH, BATCH, SEQ, D_PLE), 1.0)
    d['p_sample'] = nrm((DEPTH, DEC_BATCH, DEC_SEQ, D_PLE), 1.0)
    d['norm_ffn1'] = gain((DEPTH, D_MODEL))
    d['ffn1_wg'] = nrm((DEPTH, D_MODEL, D_FF), D_MODEL ** -0.5)
    d['ffn1_wu'] = nrm((DEPTH, D_MODEL, D_FF), D_MODEL ** -0.5)
    d['ffn1_wd'] = nrm((DEPTH, D_FF, D_MODEL), D_FF ** -0.5)
    d['norm_mix'] = gain((DEPTH, D_MODEL))
    d['norm_ffn2'] = gain((DEPTH, D_MODEL))
    d['ffn2_wg'] = nrm((DEPTH, D_MODEL, D_FF), D_MODEL ** -0.5)
    d['ffn2_wu'] = nrm((DEPTH, D_MODEL, D_FF), D_MODEL ** -0.5)
    d['ffn2_wd'] = nrm((DEPTH, D_FF, D_MODEL), D_FF ** -0.5)
    d['norm_ple'] = gain((DEPTH, D_MODEL))
    d['ple_gate'] = nrm((DEPTH, D_MODEL, D_MODEL), D_MODEL ** -0.5)
    d['ple_proj'] = nrm((DEPTH, D_PLE, D_MODEL), D_PLE ** -0.5)
    d['w_in_ab'] = nrm((N_AB, D_MODEL, COLS_AB), D_MODEL ** -0.5)
    d['mu_a'] = unif((N_AB, COLS_A), 0.0, 1.0)
    ramp = jnp.arange(W_A, dtype=f32) / (W_A - 1)
    d['w0_a'] = (-6.5 + 5.0 * ramp)[None, :] + nrm((N_AB, W_A), 0.1)
    d['w2_a'] = nrm((N_AB, LORA_W, W_A), 0.1 * LORA_W ** -0.5)
    d['a0_a'] = nrm((N_AB, W_A), 0.1)
    d['a2_a'] = nrm((N_AB, LORA_A, W_A), LORA_A ** -0.5)
    d['g2_a'] = nrm((N_AB, LORA_G, W_A), LORA_G ** -0.5)
    d['kk_a'] = 0.85 + nrm((N_AB, W_A), 0.02)
    d['ka_a'] = 1.0 + nrm((N_AB, W_A), 0.02)
    d['rk_a'] = nrm((N_AB, H_A, HEAD), 0.1)
    d['lnx_g'] = gain((N_AB, W_A))
    d['lnx_b'] = nrm((N_AB, W_A), 0.01)
    d['conv_w_b'] = nrm((N_AB, CONV_W, W_B), CONV_W ** -0.5)
    d['conv_b_b'] = nrm((N_AB, W_B), 0.01)
    d['wa_b'] = nrm((N_AB, N_BLK_B, HEAD, HEAD), HEAD ** -0.5)
    d['ba_b'] = nrm((N_AB, W_B), 0.01)
    d['wx_b'] = nrm((N_AB, N_BLK_B, HEAD, HEAD), HEAD ** -0.5)
    d['bx_b'] = nrm((N_AB, W_B), 0.01)
    a_pow = unif((N_AB, W_B), 0.9, 0.999) ** (1.0 / LRU_C)
    d['lam_b'] = jnp.log(a_pow) - jnp.log1p(-a_pow)
    d['w_out_ab'] = nrm((N_AB, W_A + W_B, D_MODEL), (W_A + W_B) ** -0.5)
    d['a_re_c'] = -0.5 * jnp.exp(nrm((N_C, N_GRP_C, P_C), 0.05))
    d['a_im_c'] = math.pi * jnp.arange(P_C, dtype=f32)[None, None, :] + nrm((N_C, N_GRP_C, P_C), 0.01)
    d['log_dt_c'] = unif((N_C, N_GRP_C), math.log(1e-3), math.log(1e-1))
    d['b_re_c'] = nrm((N_C, N_GRP_C, P_C, GRP_C), (2 * GRP_C) ** -0.5)
    d['b_im_c'] = nrm((N_C, N_GRP_C, P_C, GRP_C), (2 * GRP_C) ** -0.5)
    d['c_re_c'] = nrm((N_C, N_GRP_C, GRP_C, P_C), P_C ** -0.5)
    d['c_im_c'] = nrm((N_C, N_GRP_C, GRP_C, P_C), P_C ** -0.5)
    d['d_c'] = nrm((N_C, W_C), 1.0)
    d['w_glu_c'] = nrm((N_C, W_C, W_C), W_C ** -0.5)
    d['b_glu_c'] = nrm((N_C, W_C), 0.01)
    d['final_norm'] = gain((D_MODEL,))
    return d


def reference(x_prompt, x_sample, state_a_wkv, state_a_shift, state_b_h, state_b_conv,
              state_c_re, state_c_im, p_prompt, p_sample,
              norm_ffn1, ffn1_wg, ffn1_wu, ffn1_wd, norm_mix, norm_ffn2, ffn2_wg, ffn2_wu, ffn2_wd,
              norm_ple, ple_gate, ple_proj,
              w_in_ab, mu_a, w0_a, w2_a, a0_a, a2_a, g2_a, kk_a, ka_a, rk_a, lnx_g, lnx_b,
              conv_w_b, conv_b_b, wa_b, ba_b, wx_b, bx_b, lam_b, w_out_ab,
              a_re_c, a_im_c, log_dt_c, b_re_c, b_im_c, c_re_c, c_im_c, d_c, w_glu_c, b_glu_c,
              final_norm):
    w = dict(norm_ffn1=norm_ffn1, ffn1_wg=ffn1_wg, ffn1_wu=ffn1_wu, ffn1_wd=ffn1_wd,
             norm_mix=norm_mix, norm_ffn2=norm_ffn2, ffn2_wg=ffn2_wg, ffn2_wu=ffn2_wu,
             ffn2_wd=ffn2_wd, norm_ple=norm_ple, ple_gate=ple_gate, ple_proj=ple_proj,
             w_in_ab=w_in_ab, mu_a=mu_a, w0_a=w0_a, w2_a=w2_a, a0_a=a0_a, a2_a=a2_a, g2_a=g2_a,
             kk_a=kk_a, ka_a=ka_a, rk_a=rk_a, lnx_g=lnx_g, lnx_b=lnx_b,
             conv_w_b=conv_w_b, conv_b_b=conv_b_b, wa_b=wa_b, ba_b=ba_b, wx_b=wx_b, bx_b=bx_b,
             lam_b=lam_b, w_out_ab=w_out_ab,
             a_re_c=a_re_c, a_im_c=a_im_c, log_dt_c=log_dt_c, b_re_c=b_re_c, b_im_c=b_im_c,
             c_re_c=c_re_c, c_im_c=c_im_c, d_c=d_c, w_glu_c=w_glu_c, b_glu_c=b_glu_c,
             final_norm=final_norm)
    bp = x_prompt.shape[0]
    z_wkv = jnp.zeros((N_AB, bp, H_A, HEAD, HEAD), state_a_wkv.dtype)
    z_shift = jnp.zeros((N_AB, bp, COLS_A), state_a_shift.dtype)
    z_h = jnp.zeros((N_AB, bp, W_B), state_b_h.dtype)
    z_conv = jnp.zeros((N_AB, bp, CONV_W - 1, W_B), state_b_conv.dtype)
    z_cre = jnp.zeros((N_C, bp, N_GRP_C, P_C), state_c_re.dtype)
    z_cim = jnp.zeros((N_C, bp, N_GRP_C, P_C), state_c_im.dtype)
    y_prompt, (pw, ps, ph, pc, pre, pim) = trunk(
        x_prompt, p_prompt, z_wkv, z_shift, z_h, z_conv, z_cre, z_cim, w)
    y_sample, (sw, ss, sh, sc, sre, sim) = trunk(
        x_sample, p_sample, state_a_wkv, state_a_shift, state_b_h, state_b_conv,
        state_c_re, state_c_im, w)
    return (y_prompt, y_sample, pw, ps, ph, pc, pre, pim, sw, ss, sh, sc, sre, sim)
```

```python
import functools
import math

import jax
import jax.numpy as jnp
from jax import lax
from jax.experimental import pallas as pl
from jax.experimental.pallas import tpu as pltpu

F32 = jnp.float32
BF16 = jnp.bfloat16

HEAD = 64
LANES = 128
SUBLANES = 8
LORA_W = 64
LORA_A = 64
LORA_G = 160
CONV_W = 4
LRU_C = 8.0
GRP_C = 16
P_C = 64
RMS_EPS = 1e-6
GN_EPS = 64e-5
S5_CHUNK = 16
VMEM_LIMIT = 56 * 1024 * 1024


def _params(*sem):
    return pltpu.CompilerParams(dimension_semantics=sem, vmem_limit_bytes=VMEM_LIMIT)


def _rms(x, g):
    return x * lax.rsqrt(jnp.mean(x * x, axis=-1, keepdims=True) + RMS_EPS) * g


def _softplus(x):
    return jnp.maximum(x, 0.0) + jnp.log1p(jnp.exp(-jnp.abs(x)))


def _bdot(a, b):
    return jnp.dot(a.astype(BF16), b, preferred_element_type=F32)


def _split_dot(x, e):
    hi = x.astype(BF16)
    lo = (x - hi.astype(F32)).astype(BF16)
    return (jnp.dot(hi, e, preferred_element_type=F32)
            + jnp.dot(lo, e, preferred_element_type=F32))


def _head_sum(x, e_red, e_exp):
    return _split_dot(_split_dot(x, e_red), e_exp)


def _row_tile(n, pref):
    t = min(n, pref)
    while n % t:
        t //= 2
    return t


def _ffn_kernel(x_ref, g_ref, wg_ref, wu_ref, wd_ref, o_ref, h_ref, acc_ref):
    j = pl.program_id(1)

    @pl.when(j == 0)
    def _():
        h_ref[...] = _rms(x_ref[...], g_ref[...]).astype(BF16)
        acc_ref[...] = jnp.zeros_like(acc_ref)

    h = h_ref[...]
    a = jnp.dot(h, wg_ref[...], preferred_element_type=F32)
    b = jnp.dot(h, wu_ref[...], preferred_element_type=F32)
    hid = (a * jax.nn.sigmoid(a)) * b
    acc_ref[...] += jnp.dot(hid.astype(BF16), wd_ref[...], preferred_element_type=F32)

    @pl.when(j == pl.num_programs(1) - 1)
    def _():
        o_ref[...] = x_ref[...] + 0.5 * acc_ref[...]


def _ffn(x, g, wg, wu, wd):
    n, d = x.shape
    f = wg.shape[1]
    tm = _row_tile(n, 512)
    tf = _row_tile(f, 512)
    return pl.pallas_call(
        _ffn_kernel,
        grid=(n // tm, f // tf),
        in_specs=[
            pl.BlockSpec((tm, d), lambda i, j: (i, 0)),
            pl.BlockSpec((1, d), lambda i, j: (0, 0)),
            pl.BlockSpec((d, tf), lambda i, j: (0, j)),
            pl.BlockSpec((d, tf), lambda i, j: (0, j)),
            pl.BlockSpec((tf, d), lambda i, j: (j, 0)),
        ],
        out_specs=pl.BlockSpec((tm, d), lambda i, j: (i, 0)),
        out_shape=jax.ShapeDtypeStruct((n, d), F32),
        scratch_shapes=[pltpu.VMEM((tm, d), BF16), pltpu.VMEM((tm, d), F32)],
        compiler_params=_params("parallel", "arbitrary"),
        name="ffn",
    )(x, g, wg, wu, wd)


def _ple_kernel(x_ref, g_ref, wgate_ref, p_ref, wproj_ref, o_ref):
    x = x_ref[...]
    gate = jax.nn.sigmoid(_bdot(_rms(x, g_ref[...]), wgate_ref[...]))
    o_ref[...] = x + gate * _bdot(p_ref[...], wproj_ref[...])


def _ple(x, g, wgate, p, wproj):
    n, d = x.shape
    dp = p.shape[1]
    tm = _row_tile(n, 512)
    return pl.pallas_call(
        _ple_kernel,
        grid=(n // tm,),
        in_specs=[
            pl.BlockSpec((tm, d), lambda i: (i, 0)),
            pl.BlockSpec((1, d), lambda i: (0, 0)),
            pl.BlockSpec((d, d), lambda i: (0, 0)),
            pl.BlockSpec((tm, dp), lambda i: (i, 0)),
            pl.BlockSpec((dp, d), lambda i: (0, 0)),
        ],
        out_specs=pl.BlockSpec((tm, d), lambda i: (i, 0)),
        out_shape=jax.ShapeDtypeStruct((n, d), F32),
        compiler_params=_params("parallel"),
        name="ple",
    )(x, g, wgate, p, wproj)


def _rmsnorm_kernel(x_ref, g_ref, o_ref):
    o_ref[...] = _rms(x_ref[...], g_ref[...])


def _rmsnorm(x, g):
    n, d = x.shape
    tm = _row_tile(n, 512)
    return pl.pallas_call(
        _rmsnorm_kernel,
        grid=(n // tm,),
        in_specs=[pl.BlockSpec((tm, d), lambda i: (i, 0)),
                  pl.BlockSpec((1, d), lambda i: (0, 0))],
        out_specs=pl.BlockSpec((tm, d), lambda i: (i, 0)),
        out_shape=jax.ShapeDtypeStruct((n, d), F32),
        compiler_params=_params("parallel"),
        name="rmsnorm",
    )(x, g)


def _inproj_kernel(x_ref, g_ref, w_ref, o_ref, h_ref):
    @pl.when(pl.program_id(1) == 0)
    def _():
        h_ref[...] = _rms(x_ref[...], g_ref[...]).astype(BF16)

    o_ref[...] = jnp.dot(h_ref[...], w_ref[...], preferred_element_type=F32)


def _inproj(x, g, w):
    n, d = x.shape
    c = w.shape[1]
    tm = _row_tile(n, 512)
    tn = _row_tile(c, 512)
    return pl.pallas_call(
        _inproj_kernel,
        grid=(n // tm, c // tn),
        in_specs=[
            pl.BlockSpec((tm, d), lambda i, j: (i, 0)),
            pl.BlockSpec((1, d), lambda i, j: (0, 0)),
            pl.BlockSpec((d, tn), lambda i, j: (0, j)),
        ],
        out_specs=pl.BlockSpec((tm, tn), lambda i, j: (i, j)),
        out_shape=jax.ShapeDtypeStruct((n, c), F32),
        scratch_shapes=[pltpu.VMEM((tm, d), BF16)],
        compiler_params=_params("parallel", "arbitrary"),
        name="inproj",
    )(x, g, w)


def _rwkv_prep_kernel(z_ref, shift_ref, mu_ref, w0_ref, w2_ref, a0_ref, a2_ref, g2_ref,
                      kk_ref, ka_ref, ered_ref, eexp_ref,
                      r_ref, w_ref, k_ref, v_ref, nkk_ref, kka_ref, g_ref, carry_ref, *, wa):
    tb = pl.program_id(1)
    z = z_ref[...]
    bb, tt, ca = z.shape

    @pl.when(tb == 0)
    def _():
        carry_ref[...] = shift_ref[...]

    t_idx = lax.broadcasted_iota(jnp.int32, z.shape, 1)
    zprev = jnp.where(t_idx == 0, carry_ref[...], pltpu.roll(z, 1, axis=1))
    carry_ref[...] = z[:, tt - 1:tt, :]
    zs = (z + (zprev - z) * mu_ref[...]).reshape(bb * tt, ca)

    r = zs[:, 0:wa]
    k = zs[:, wa:2 * wa]
    v = zs[:, 2 * wa:3 * wa]
    xwa = zs[:, 3 * wa:3 * wa + LANES]
    xg = zs[:, 3 * wa + LANES:3 * wa + 3 * LANES]
    w_log = -_softplus(-(w0_ref[...] + _bdot(jnp.tanh(xwa), w2_ref[...]))) - 0.5
    decay = jnp.exp(-jnp.exp(w_log))
    a = jax.nn.sigmoid(a0_ref[...] + _bdot(xwa, a2_ref[...]))
    g = _bdot(jax.nn.sigmoid(xg), g2_ref[...])
    kk = k * kk_ref[...]
    kn = kk / jnp.maximum(jnp.sqrt(_head_sum(kk * kk, ered_ref[...], eexp_ref[...])), 1e-12)
    k2 = k * (1.0 + (a - 1.0) * ka_ref[...])
    shp = (bb, tt, wa)
    r_ref[...] = r.reshape(shp)
    w_ref[...] = decay.reshape(shp)
    k_ref[...] = k2.reshape(shp)
    v_ref[...] = v.reshape(shp)
    nkk_ref[...] = (-kn).reshape(shp)
    kka_ref[...] = (kn * a).reshape(shp)
    g_ref[...] = g.reshape(shp)


def _rwkv_prep(z3, shift, mu, w0, w2p, a0, a2p, g2p, kk, ka, ered, eexp, wa, ca, bb, tt):
    b, t, _ = z3.shape
    row = lambda i, j: (0, 0)
    vec = lambda c: pl.BlockSpec((1, c), row)
    out_spec = pl.BlockSpec((bb, tt, wa), lambda i, j: (i, j, 0))
    out_shape = jax.ShapeDtypeStruct((b, t, wa), F32)
    return pl.pallas_call(
        functools.partial(_rwkv_prep_kernel, wa=wa),
        grid=(b // bb, t // tt),
        in_specs=[
            pl.BlockSpec((bb, tt, ca), lambda i, j: (i, j, 0)),
            pl.BlockSpec((bb, 1, ca), lambda i, j: (i, 0, 0)),
            vec(ca), vec(wa),
            pl.BlockSpec(w2p.shape, row), vec(wa),
            pl.BlockSpec(a2p.shape, row), pl.BlockSpec(g2p.shape, row),
            vec(wa), vec(wa),
            pl.BlockSpec(ered.shape, row), pl.BlockSpec(eexp.shape, row),
        ],
        out_specs=[out_spec] * 7,
        out_shape=[out_shape] * 7,
        scratch_shapes=[pltpu.VMEM((bb, 1, ca), F32)],
        compiler_params=_params("parallel", "arbitrary"),
        name="rwkv_prep",
    )(z3, shift, mu, w0, w2p, a0, a2p, g2p, kk, ka, ered, eexp)


def _wkv_kernel(r_ref, w_ref, k_ref, v_ref, nkk_ref, kka_ref, s0_ref, y_ref, s_ref):
    tb = pl.program_id(1)
    bb, tt, wa = r_ref.shape
    n_hp = wa // LANES

    @pl.when(tb == 0)
    def _():
        s_ref[...] = s0_ref[...]

    shape = (HEAD, LANES)
    lane = lax.broadcasted_iota(jnp.int32, shape, 1)
    sub = lax.broadcasted_iota(jnp.int32, shape, 0)
    low = lane < HEAD
    diag0 = lane == sub
    diag1 = lane == sub + HEAD
    diag = jnp.logical_or(diag0, diag1)

    def pair_sum(p):
        s_lo = jnp.sum(jnp.where(low, p, 0.0), axis=1, keepdims=True)
        s_hi = jnp.sum(jnp.where(low, 0.0, p), axis=1, keepdims=True)
        return jnp.where(low, s_lo, s_hi)

    sub8 = lax.broadcasted_iota(jnp.int32, (SUBLANES, LANES), 0)

    def tile_step(it, carry):
        b = it // (tt // SUBLANES)
        off = pl.multiple_of((it % (tt // SUBLANES)) * SUBLANES, SUBLANES)
        for hp in range(n_hp):
            ls = slice(hp * LANES, (hp + 1) * LANES)
            r8, w8, k8, v8, nkk8, kka8 = (
                ref[b, pl.ds(off, SUBLANES), ls] for ref in (r_ref, w_ref, k_ref, v_ref, nkk_ref, kka_ref))
            s = s_ref[b, hp]
            y8 = jnp.zeros((SUBLANES, LANES), F32)
            for i in range(SUBLANES):
                row = lambda x: x[i:i + 1, :]
                sa = pair_sum(s * row(nkk8))
                v_row = jnp.broadcast_to(row(v8), shape)
                v_lo = jnp.sum(jnp.where(diag0, v_row, 0.0), axis=1, keepdims=True)
                v_hi = jnp.sum(jnp.where(diag1, v_row, 0.0), axis=1, keepdims=True)
                v_col = jnp.where(low, v_lo, v_hi)
                s = s * row(w8) + sa * row(kka8) + v_col * row(k8)
                y_col = pair_sum(s * row(r8))
                y_row = jnp.sum(jnp.where(diag, y_col, 0.0), axis=0, keepdims=True)
                y8 = jnp.where(sub8 == i, y_row, y8)
            s_ref[b, hp] = s
            y_ref[b, pl.ds(off, SUBLANES), ls] = y8
        return carry

    lax.fori_loop(0, bb * (tt // SUBLANES), tile_step, 0)


def _wkv(r, w, k, v, nkk, kka, s0, bb, tt):
    b, t, wa = r.shape
    n_hp = wa // LANES
    seq = pl.BlockSpec((bb, tt, wa), lambda i, j: (i, j, 0))
    st = pl.BlockSpec((bb, n_hp, HEAD, LANES), lambda i, j: (i, 0, 0, 0))
    return pl.pallas_call(
        _wkv_kernel,
        grid=(b // bb, t // tt),
        in_specs=[seq] * 6 + [st],
        out_specs=[seq, st],
        out_shape=[jax.ShapeDtypeStruct((b, t, wa), F32),
                   jax.ShapeDtypeStruct((b, n_hp, HEAD, LANES), F32)],
        compiler_params=_params("parallel", "arbitrary"),
        name="wkv",
    )(r, w, k, v, nkk, kka, s0)


def _mix_out_kernel(x_ref, y_ref, r_ref, k_ref, v_ref, g_ref, yb_ref, rk_ref, lng_ref, lnb_ref,
                    ered_ref, eexp_ref, woa_ref, wob_ref, o_ref):
    ered = ered_ref[...]
    eexp = eexp_ref[...]
    y = y_ref[...]
    v = v_ref[...]
    mean = _head_sum(y, ered, eexp) * (1.0 / HEAD)
    d = y - mean
    var = _head_sum(d * d, ered, eexp) * (1.0 / HEAD)
    yn = d * lax.rsqrt(var + GN_EPS) * lng_ref[...] + lnb_ref[...]
    bonus = _head_sum(r_ref[...] * k_ref[...] * rk_ref[...], ered, eexp) * v
    ya = (yn + bonus) * g_ref[...]
    o_ref[...] = x_ref[...] + _bdot(ya, woa_ref[...]) + _bdot(yb_ref[...], wob_ref[...])


def _mix_out(x, y, r, k, v, g, yb, rk, lng, lnb, ered, eexp, woa, wob):
    n, d = x.shape
    wa = y.shape[1]
    wb = yb.shape[1]
    tm = _row_tile(n, 256)
    row = lambda i: (0, 0)
    ta = pl.BlockSpec((tm, wa), lambda i: (i, 0))
    va = pl.BlockSpec((1, wa), row)
    return pl.pallas_call(
        _mix_out_kernel,
        grid=(n // tm,),
        in_specs=[pl.BlockSpec((tm, d), lambda i: (i, 0)), ta, ta, ta, ta, ta,
                  pl.BlockSpec((tm, wb), lambda i: (i, 0)), va, va, va,
                  pl.BlockSpec(ered.shape, row), pl.BlockSpec(eexp.shape, row),
                  pl.BlockSpec(woa.shape, row), pl.BlockSpec(wob.shape, row)],
        out_specs=pl.BlockSpec((tm, d), lambda i: (i, 0)),
        out_shape=jax.ShapeDtypeStruct((n, d), F32),
        compiler_params=_params("parallel"),
        name="mix_out",
    )(x, y, r, k, v, g, yb, rk, lng, lnb, ered, eexp, woa, wob)


def _rglru_kernel(xb_ref, gb_ref, h0_ref, hist0_ref, cw_ref, cb_ref, wa_ref, ba_ref, wx_ref, bx_ref,
                  lam_ref, y_ref, hl_ref, hist_ref, h_ref, a_scr, b_scr):
    tb = pl.program_id(1)
    x = xb_ref[...]
    bb, tt, wb = x.shape
    n_tile = tt // SUBLANES

    @pl.when(tb == 0)
    def _():
        hist_ref[...] = hist0_ref[...]
        h_ref[...] = h0_ref[...]

    hist = hist_ref[...]
    hist_ref[...] = x[:, tt - SUBLANES:tt, :]
    cw = cw_ref[...]
    t8 = lax.broadcasted_iota(jnp.int32, (bb, SUBLANES, wb), 1)
    xc = cb_ref[...] + x * cw[CONV_W - 1:CONV_W, :]
    for dly in range(1, CONV_W):
        rolled = pltpu.roll(x, dly, axis=1)
        head = jnp.where(t8 < dly, pltpu.roll(hist, dly, axis=1), rolled[:, :SUBLANES, :])
        if n_tile > 1:
            shifted = jnp.concatenate([head, rolled[:, SUBLANES:, :]], axis=1)
        else:
            shifted = head
        xc = xc + shifted * cw[CONV_W - 1 - dly:CONV_W - dly, :]

    xc2 = xc.reshape(bb * tt, wb)
    xcb = xc2.astype(BF16)
    n_q = wa_ref.shape[0]
    wq = wb // n_q
    gr = jnp.concatenate(
        [jnp.dot(xcb[:, q * wq:(q + 1) * wq], wa_ref[q], preferred_element_type=F32) for q in range(n_q)],
        axis=1)
    gi = jnp.concatenate(
        [jnp.dot(xcb[:, q * wq:(q + 1) * wq], wx_ref[q], preferred_element_type=F32) for q in range(n_q)],
        axis=1)
    gate_r = jax.nn.sigmoid(gr + ba_ref[...])
    gate_i = jax.nn.sigmoid(gi + bx_ref[...])
    log_a = (-LRU_C) * gate_r * _softplus(-lam_ref[...])
    a = jnp.exp(log_a)
    b = jnp.sqrt(1.0 - jnp.exp(2.0 * log_a)) * (gate_i * xc2)
    a_scr[...] = a.reshape(bb, tt, wb)
    b_scr[...] = b.reshape(bb, tt, wb)

    def tile_scan(i, h):
        off = pl.multiple_of(i * SUBLANES, SUBLANES)
        at = a_scr[:, pl.ds(off, SUBLANES), :]
        bt = b_scr[:, pl.ds(off, SUBLANES), :]
        for dly in (1, 2, 4):
            keep = t8 >= dly
            bt = jnp.where(keep, at * pltpu.roll(bt, dly, axis=1) + bt, bt)
            at = jnp.where(keep, at * pltpu.roll(at, dly, axis=1), at)
        ht = bt + at * h
        b_scr[:, pl.ds(off, SUBLANES), :] = ht
        return ht[:, SUBLANES - 1:SUBLANES, :]

    h_last = lax.fori_loop(0, n_tile, tile_scan, h_ref[...])
    h_ref[...] = h_last
    hl_ref[...] = h_last
    y_ref[...] = b_scr[...] * jax.nn.gelu(gb_ref[...])


def _rglru(zb3, h0, hist0, cw, cb, wa4, ba, wx4, bx, lam, bb, tt):
    b, t, _ = zb3.shape
    wb = h0.shape[-1]
    row = lambda i, j: (0, 0)
    vec = pl.BlockSpec((1, wb), row)
    st = pl.BlockSpec((bb, 1, wb), lambda i, j: (i, 0, 0))
    return pl.pallas_call(
        _rglru_kernel,
        grid=(b // bb, t // tt),
        in_specs=[
            pl.BlockSpec((bb, tt, wb), lambda i, j: (i, j, 0)),
            pl.BlockSpec((bb, tt, wb), lambda i, j: (i, j, 1)),
            st,
            pl.BlockSpec((bb, SUBLANES, wb), lambda i, j: (i, 0, 0)),
            pl.BlockSpec((CONV_W, wb), row), vec,
            pl.BlockSpec(wa4.shape, lambda i, j: (0, 0, 0)), vec,
            pl.BlockSpec(wx4.shape, lambda i, j: (0, 0, 0)), vec, vec,
        ],
        out_specs=[pl.BlockSpec((bb, tt, wb), lambda i, j: (i, j, 0)), st],
        out_shape=[jax.ShapeDtypeStruct((b, t, wb), F32), jax.ShapeDtypeStruct((b, 1, wb), F32)],
        scratch_shapes=[pltpu.VMEM((bb, SUBLANES, wb), F32), pltpu.VMEM((bb, 1, wb), F32),
                        pltpu.VMEM((bb, tt, wb), F32), pltpu.VMEM((bb, tt, wb), F32)],
        compiler_params=_params("parallel", "arbitrary"),
        name="rglru",
    )(zb3, zb3, h0, hist0, cw, cb, wa4, ba, wx4, bx, lam)


def _s5_setup_kernel(are_ref, aim_ref, ldt_ref, bre_ref, bim_ref, cre_ref, cim_ref,
                     k_ref, mre_ref, mim_ref, pbre_ref, pbim_ref, pre_ref, pim_ref):
    a_re = are_ref[...]
    a_im = aim_ref[...]
    dt = jnp.exp(ldt_ref[...])
    mag = jnp.exp(dt * a_re)
    ab_re = mag * jnp.cos(dt * a_im)
    ab_im = mag * jnp.sin(dt * a_im)
    den = a_re * a_re + a_im * a_im
    f_re = ((ab_re - 1.0) * a_re + ab_im * a_im) / den
    f_im = (ab_im * a_re - (ab_re - 1.0) * a_im) / den
    b_re = bre_ref[...]
    b_im = bim_ref[...]
    bb_re = f_re[:, None, :] * b_re - f_im[:, None, :] * b_im
    bb_im = f_re[:, None, :] * b_im + f_im[:, None, :] * b_re
    c_re = cre_ref[...]
    c_im = cim_ref[...]
    rhs = jnp.concatenate([bb_re, bb_im], axis=2)
    p_re = jnp.ones_like(a_re)
    p_im = jnp.zeros_like(a_re)
    for tau in range(S5_CHUNK + 1):
        pre_ref[:, tau, :] = p_re
        pim_ref[:, tau, :] = p_im
        m_re = c_re * p_re[:, None, :] - c_im * p_im[:, None, :]
        m_im = c_re * p_im[:, None, :] + c_im * p_re[:, None, :]
        mre_ref[:, tau] = m_re
        mim_ref[:, tau] = m_im
        pbre_ref[:, tau] = p_re[:, None, :] * bb_re - p_im[:, None, :] * bb_im
        pbim_ref[:, tau] = p_re[:, None, :] * bb_im + p_im[:, None, :] * bb_re
        lhs = jnp.concatenate([m_re, -m_im], axis=2)
        k_ref[:, tau] = jnp.einsum("gmk,gnk->gmn", lhs, rhs, precision=lax.Precision.HIGHEST,
                                   preferred_element_type=F32)
        p_re, p_im = p_re * ab_re - p_im * ab_im, p_re * ab_im + p_im * ab_re


def _s5_setup(a_re, a_im, log_dt, b_re_t, b_im_t, c_re, c_im):
    g, p = a_re.shape
    c = c_re.shape[1]
    gb = 8
    nt = S5_CHUNK + 1
    g2 = pl.BlockSpec((gb, p), lambda i: (i, 0))
    g3 = pl.BlockSpec((gb, c, p), lambda i: (i, 0, 0))
    o4 = pl.BlockSpec((gb, nt, c, p), lambda i: (i, 0, 0, 0))
    o3 = pl.BlockSpec((gb, nt, p), lambda i: (i, 0, 0))
    s4 = jax.ShapeDtypeStruct((g, nt, c, p), F32)
    s3 = jax.ShapeDtypeStruct((g, nt, p), F32)
    return pl.pallas_call(
        _s5_setup_kernel,
        grid=(g // gb,),
        in_specs=[g2, g2, pl.BlockSpec((gb, 1), lambda i: (i, 0)), g3, g3, g3, g3],
        out_specs=[pl.BlockSpec((gb, nt, c, c), lambda i: (i, 0, 0, 0)), o4, o4, o4, o4, o3, o3],
        out_shape=[jax.ShapeDtypeStruct((g, nt, c, c), F32), s4, s4, s4, s4, s3, s3],
        compiler_params=_params("parallel"),
        name="s5_setup",
    )(a_re, a_im, log_dt, b_re_t, b_im_t, c_re, c_im)


def _s5_kernel(u_ref, h0r_ref, h0i_ref, kt_ref, gm_ref, hm_ref, ar_ref, ai_ref,
               y_ref, her_ref, hei_ref, gur, gui, hsr, hsi, *, n_chunk, rows):
    n_grp = kt_ref.shape[0]
    lc = kt_ref.shape[1]
    n_pair = n_grp // 2
    ub = u_ref[...].astype(BF16)
    for q in range(n_pair):
        gu = jnp.dot(ub[:, 2 * q * lc:(2 * q + 2) * lc], gm_ref[q], preferred_element_type=F32)
        gur[:, q * LANES:(q + 1) * LANES] = gu[:, :LANES]
        gui[:, q * LANES:(q + 1) * LANES] = gu[:, LANES:]
    a_r = ar_ref[...]
    a_i = ai_ref[...]

    def chunk_step(c, h):
        h_r, h_i = h
        off = pl.multiple_of(c * rows, SUBLANES)
        hsr[pl.ds(off, rows), :] = h_r
        hsi[pl.ds(off, rows), :] = h_i
        n_r = a_r * h_r - a_i * h_i + gur[pl.ds(off, rows), :]
        n_i = a_r * h_i + a_i * h_r + gui[pl.ds(off, rows), :]
        return n_r, n_i

    h_r, h_i = lax.fori_loop(0, n_chunk, chunk_step, (h0r_ref[...], h0i_ref[...]))
    her_ref[...] = h_r
    hei_ref[...] = h_i
    for q in range(n_pair):
        hs = jnp.concatenate([hsr[:, q * LANES:(q + 1) * LANES], hsi[:, q * LANES:(q + 1) * LANES]], axis=1)
        carry = jnp.dot(hs.astype(BF16), hm_ref[q], preferred_element_type=F32)
        for s in range(2):
            g = 2 * q + s
            y_ref[:, g * lc:(g + 1) * lc] = (
                jnp.dot(ub[:, g * lc:(g + 1) * lc], kt_ref[g], preferred_element_type=F32)
                + carry[:, s * lc:(s + 1) * lc])


def _s5_conv(u, h0r, h0i, kt, gm2, hm2, ar, ai, n_chunk, rows):
    n, _ = u.shape
    g, lc, _ = kt.shape
    gb = 8
    pb = gb // 2
    sw = pb * LANES
    return pl.pallas_call(
        functools.partial(_s5_kernel, n_chunk=n_chunk, rows=rows),
        grid=(g // gb,),
        in_specs=[
            pl.BlockSpec((n, gb * lc), lambda i: (0, i)),
            pl.BlockSpec((rows, sw), lambda i: (0, i)),
            pl.BlockSpec((rows, sw), lambda i: (0, i)),
            pl.BlockSpec((gb, lc, lc), lambda i: (i, 0, 0)),
            pl.BlockSpec((pb, 2 * lc, 2 * LANES), lambda i: (i, 0, 0)),
            pl.BlockSpec((pb, 2 * LANES, 2 * lc), lambda i: (i, 0, 0)),
            pl.BlockSpec((1, sw), lambda i: (0, i)),
            pl.BlockSpec((1, sw), lambda i: (0, i)),
        ],
        out_specs=[pl.BlockSpec((n, gb * lc), lambda i: (0, i)),
                   pl.BlockSpec((rows, sw), lambda i: (0, i)),
                   pl.BlockSpec((rows, sw), lambda i: (0, i))],
        out_shape=[jax.ShapeDtypeStruct(u.shape, F32),
                   jax.ShapeDtypeStruct(h0r.shape, F32), jax.ShapeDtypeStruct(h0i.shape, F32)],
        scratch_shapes=[pltpu.VMEM((n, sw), F32)] * 4,
        compiler_params=_params("parallel"),
        name="s5_conv",
    )(u, h0r, h0i, kt, gm2, hm2, ar, ai)


def _s5_out_kernel(x_ref, u_ref, yc_ref, d_ref, w_ref, b_ref, o_ref):
    z = jax.nn.gelu(yc_ref[...] + d_ref[...] * u_ref[...])
    o_ref[...] = x_ref[...] + z * jax.nn.sigmoid(_bdot(z, w_ref[...]) + b_ref[...])


def _s5_out(x, u, yc, d, w, b):
    n, dm = x.shape
    tm = _row_tile(n, 256)
    row = lambda i: (0, 0)
    tile = pl.BlockSpec((tm, dm), lambda i: (i, 0))
    vec = pl.BlockSpec((1, dm), row)
    return pl.pallas_call(
        _s5_out_kernel,
        grid=(n // tm,),
        in_specs=[tile, tile, tile, vec, pl.BlockSpec((dm, dm), row), vec],
        out_specs=tile,
        out_shape=jax.ShapeDtypeStruct((n, dm), F32),
        compiler_params=_params("parallel"),
        name="s5_out",
    )(x, u, yc, d, w, b)


def _block_diag_tiles(w, per_tile):
    n_blk, h, _ = w.shape
    n_tile = n_blk // per_tile
    w = w.reshape(n_tile, per_tile, h, h)
    eye = jnp.eye(per_tile, dtype=w.dtype)
    out = jnp.einsum("tphk,pq->tphqk", w, eye)
    return out.reshape(n_tile, per_tile * h, per_tile * h)


def _head_sum_mats(wa):
    n_head = wa // HEAD
    onehot = (jnp.arange(wa)[:, None] // HEAD == jnp.arange(LANES)[None, :])
    return onehot.astype(BF16), onehot.T.astype(BF16)


def _s5_mats(k_tab, m_re, m_im, pb_re, pb_im, p_re, p_im, chunk):
    g, _, c, p = m_re.shape
    lc = chunk * c
    s = jnp.arange(chunk)
    tau = s[None, :] - s[:, None]
    tau = jnp.where(tau >= 0, tau, S5_CHUNK + 1)
    k_pad = jnp.concatenate([k_tab, jnp.zeros((g, 1, c, c), F32)], axis=1)
    kt = k_pad[:, tau]
    kt = kt.transpose(0, 1, 4, 2, 3).reshape(g, lc, lc)
    rev = chunk - 1 - s
    gm_re = pb_re[:, rev].reshape(g, lc, p)
    gm_im = pb_im[:, rev].reshape(g, lc, p)
    hm_re = m_re[:, 1:chunk + 1].transpose(0, 3, 1, 2).reshape(g, p, lc)
    hm_im = -m_im[:, 1:chunk + 1].transpose(0, 3, 1, 2).reshape(g, p, lc)
    z_gm = jnp.zeros_like(gm_re)
    z_hm = jnp.zeros_like(hm_re)
    ev, od = slice(0, None, 2), slice(1, None, 2)
    gm2 = jnp.concatenate([
        jnp.concatenate([gm_re[ev], z_gm[ev], gm_im[ev], z_gm[ev]], axis=2),
        jnp.concatenate([z_gm[od], gm_re[od], z_gm[od], gm_im[od]], axis=2)], axis=1)
    hm2 = jnp.concatenate([
        jnp.concatenate([hm_re[ev], z_hm[ev]], axis=2),
        jnp.concatenate([z_hm[od], hm_re[od]], axis=2),
        jnp.concatenate([hm_im[ev], z_hm[ev]], axis=2),
        jnp.concatenate([z_hm[od], hm_im[od]], axis=2)], axis=1)
    ar = p_re[:, chunk].reshape(1, g * p)
    ai = p_im[:, chunk].reshape(1, g * p)
    return kt.astype(BF16), gm2.astype(BF16), hm2.astype(BF16), ar, ai


def _seq_blocks(b, t):
    if t >= 256:
        return 1, 256
    return min(b, max(1, 256 // t)), t


def _trunk(x, p, st_wkv, st_shift, st_h, st_conv, st_cre, st_cim, w):
    b, t, d = x.shape
    n = b * t
    depth = w["norm_ffn1"].shape[0]
    wa = w["w0_a"].shape[1]
    wb = w["lam_b"].shape[1]
    cols_a = w["mu_a"].shape[1]
    ca = ((cols_a + 511) // 512) * 512
    n_head = wa // HEAD
    n_hp = wa // LANES
    n_grp = w["a_re_c"].shape[1]
    chunk = min(S5_CHUNK, t)
    n_chunk = t // chunk
    lc = chunk * GRP_C
    bpad = ((b + SUBLANES - 1) // SUBLANES) * SUBLANES
    ered, eexp = _head_sum_mats(wa)
    row = lambda v: v.reshape(1, -1)
    bf = lambda v: v.astype(BF16)

    x = x.reshape(n, d)
    new = {k: [] for k in ("wkv", "shift", "h", "conv", "cre", "cim")}
    for l in range(depth):
        j = l // 2
        x = _ffn(x, row(w["norm_ffn1"][l]), bf(w["ffn1_wg"][l]), bf(w["ffn1_wu"][l]), bf(w["ffn1_wd"][l]))
        if l % 2 == 0:
            w_in = w["w_in_ab"][j]
            pad_a = lambda v: jnp.pad(v, [(0, 0)] * (v.ndim - 1) + [(0, ca - cols_a)])
            z3 = _inproj(x, row(w["norm_mix"][l]), bf(pad_a(w_in[:, :cols_a]))).reshape(b, t, ca)
            zb3 = _inproj(x, row(w["norm_mix"][l]), bf(w_in[:, cols_a:])).reshape(b, t, 2 * wb)
            shift = pad_a(st_shift[j])[:, None, :]
            w2p = jnp.zeros((LANES, wa), F32).at[:LORA_W].set(w["w2_a"][j])
            a2p = jnp.zeros((LANES, wa), F32).at[LORA_W:LORA_W + LORA_A].set(w["a2_a"][j])
            g2p = jnp.zeros((2 * LANES, wa), F32).at[:LORA_G].set(w["g2_a"][j])
            bb, tt = _seq_blocks(b, t)
            r, dec, k2, v, nkk, kka, g = _rwkv_prep(
                z3, shift, row(pad_a(w["mu_a"][j])), row(w["w0_a"][j]), bf(w2p), row(w["a0_a"][j]),
                bf(a2p), bf(g2p), row(w["kk_a"][j]), row(w["ka_a"][j]), ered, eexp, wa, ca, bb, tt)
            s0 = st_wkv[j].reshape(b, n_hp, 2, HEAD, HEAD).transpose(0, 1, 3, 2, 4).reshape(b, n_hp, HEAD, LANES)
            bbw = math.gcd(b, 4)
            y, s_new = _wkv(r, dec, k2, v, nkk, kka, s0, bbw, min(t, 128))
            s_new = s_new.reshape(b, n_hp, HEAD, 2, HEAD).transpose(0, 1, 3, 2, 4).reshape(b, n_head, HEAD, HEAD)
            hist0 = jnp.pad(st_conv[j], ((0, 0), (SUBLANES - (CONV_W - 1), 0), (0, 0)))
            n_q = wb // (2 * LANES)
            wa4 = bf(_block_diag_tiles(w["wa_b"][j], (wb // HEAD) // n_q))
            wx4 = bf(_block_diag_tiles(w["wx_b"][j], (wb // HEAD) // n_q))
            yb, h_new = _rglru(
                zb3, st_h[j][:, None, :], hist0, w["conv_w_b"][j], row(w["conv_b_b"][j]), wa4,
                row(w["ba_b"][j]), wx4, row(w["bx_b"][j]), row(w["lam_b"][j]), bb, tt)
            w_out = bf(w["w_out_ab"][j])
            f2 = lambda v: v.reshape(n, -1)
            x = _mix_out(x, f2(y), f2(r), f2(k2), f2(v), f2(g), f2(yb), row(w["rk_a"][j]),
                         row(w["lnx_g"][j]), row(w["lnx_b"][j]), ered, eexp, w_out[:wa], w_out[wa:])
            new["wkv"].append(s_new)
            new["shift"].append(z3[:, t - 1, :cols_a])
            new["h"].append(h_new[:, 0, :])
            conv_all = jnp.concatenate([st_conv[j], zb3[:, max(t - (CONV_W - 1), 0):, :wb]], axis=1)
            new["conv"].append(conv_all[:, conv_all.shape[1] - (CONV_W - 1):])
        else:
            u = _rmsnorm(x, row(w["norm_mix"][l]))
            tabs = _s5_setup(w["a_re_c"][j], w["a_im_c"][j], w["log_dt_c"][j][:, None],
                             w["b_re_c"][j].transpose(0, 2, 1), w["b_im_c"][j].transpose(0, 2, 1),
                             w["c_re_c"][j], w["c_im_c"][j])
            kt, gm2, hm2, ar, ai = _s5_mats(*tabs, chunk)
            ul = u.reshape(b, n_chunk, chunk, n_grp, GRP_C).transpose(1, 0, 3, 2, 4)
            ul = jnp.pad(ul, ((0, 0), (0, bpad - b), (0, 0), (0, 0), (0, 0))).reshape(n_chunk * bpad, n_grp * lc)
            pair = lambda h: jnp.pad(h.reshape(b, n_grp * P_C), ((0, bpad - b), (0, 0)))
            yl, he_r, he_i = _s5_conv(ul, pair(st_cre[j]), pair(st_cim[j]), kt, gm2, hm2, ar, ai, n_chunk, bpad)
            yc = yl.reshape(n_chunk, bpad, n_grp, chunk, GRP_C)[:, :b].transpose(1, 0, 3, 2, 4).reshape(n, d)
            x = _s5_out(x, u, yc, row(w["d_c"][j]), bf(w["w_glu_c"][j]), row(w["b_glu_c"][j]))
            new["cre"].append(he_r[:b].reshape(b, n_grp, P_C))
            new["cim"].append(he_i[:b].reshape(b, n_grp, P_C))
        x = _ffn(x, row(w["norm_ffn2"][l]), bf(w["ffn2_wg"][l]), bf(w["ffn2_wu"][l]), bf(w["ffn2_wd"][l]))
        x = _ple(x, row(w["norm_ple"][l]), bf(w["ple_gate"][l]), p[l].reshape(n, -1), bf(w["ple_proj"][l]))
    y = _rmsnorm(x, row(w["final_norm"])).reshape(b, t, d)
    stk = lambda name, ref: jnp.stack(new[name]).astype(ref.dtype)
    return y, (stk("wkv", st_wkv), stk("shift", st_shift), stk("h", st_h),
               stk("conv", st_conv), stk("cre", st_cre), stk("cim", st_cim))


def kernel(x_prompt, x_sample, state_a_wkv, state_a_shift, state_b_h, state_b_conv, state_c_re, state_c_im, p_prompt, p_sample, norm_ffn1, ffn1_wg, ffn1_wu, ffn1_wd, norm_mix, norm_ffn2, ffn2_wg, ffn2_wu, ffn2_wd, norm_ple, ple_gate, ple_proj, w_in_ab, mu_a, w0_a, w2_a, a0_a, a2_a, g2_a, kk_a, ka_a, rk_a, lnx_g, lnx_b, conv_w_b, conv_b_b, wa_b, ba_b, wx_b, bx_b, lam_b, w_out_ab, a_re_c, a_im_c, log_dt_c, b_re_c, b_im_c, c_re_c, c_im_c, d_c, w_glu_c, b_glu_c, final_norm):
    w = dict(norm_ffn1=norm_ffn1, ffn1_wg=ffn1_wg, ffn1_wu=ffn1_wu, ffn1_wd=ffn1_wd,
             norm_mix=norm_mix, norm_ffn2=norm_ffn2, ffn2_wg=ffn2_wg, ffn2_wu=ffn2_wu,
             ffn2_wd=ffn2_wd, norm_ple=norm_ple, ple_gate=ple_gate, ple_proj=ple_proj,
             w_in_ab=w_in_ab, mu_a=mu_a, w0_a=w0_a, w2_a=w2_a, a0_a=a0_a, a2_a=a2_a, g2_a=g2_a,
             kk_a=kk_a, ka_a=ka_a, rk_a=rk_a.reshape(rk_a.shape[0], -1), lnx_g=lnx_g, lnx_b=lnx_b,
             conv_w_b=conv_w_b, conv_b_b=conv_b_b, wa_b=wa_b, ba_b=ba_b, wx_b=wx_b, bx_b=bx_b,
             lam_b=lam_b, w_out_ab=w_out_ab,
             a_re_c=a_re_c, a_im_c=a_im_c, log_dt_c=log_dt_c, b_re_c=b_re_c, b_im_c=b_im_c,
             c_re_c=c_re_c, c_im_c=c_im_c, d_c=d_c, w_glu_c=w_glu_c, b_glu_c=b_glu_c,
             final_norm=final_norm)
    bp = x_prompt.shape[0]
    zeros = lambda s: jnp.zeros((s.shape[0], bp) + s.shape[2:], s.dtype)
    y_prompt, prompt_state = _trunk(
        x_prompt, p_prompt, zeros(state_a_wkv), zeros(state_a_shift), zeros(state_b_h),
        zeros(state_b_conv), zeros(state_c_re), zeros(state_c_im), w)
    y_sample, sample_state = _trunk(
        x_sample, p_sample, state_a_wkv, state_a_shift, state_b_h, state_b_conv,
        state_c_re, state_c_im, w)
    return (y_prompt, y_sample) + tuple(prompt_state) + tuple(sample_state)
```

```python
import functools

import jax
import jax.numpy as jnp
from jax import lax
from jax.experimental import pallas as pl
from jax.experimental.pallas import tpu as pltpu

F32 = jnp.float32
BF16 = jnp.bfloat16

HEAD = 64
LANES = 128
SUBLANES = 8
LORA_W = 64
LORA_A = 64
LORA_G = 160
CONV_W = 4
LRU_C = 8.0
GRP_C = 16
P_C = 64
RMS_EPS = 1e-6
GN_EPS = 64e-5
S5_CHUNK = 16
N_ACC = 4
VMEM_LIMIT = 56 * 1024 * 1024


def _params(*sem):
    return pltpu.CompilerParams(dimension_semantics=sem, vmem_limit_bytes=VMEM_LIMIT)


def _rms(x, g):
    return x * lax.rsqrt(jnp.mean(x * x, axis=-1, keepdims=True) + RMS_EPS) * g


def _softplus(x):
    return jnp.maximum(x, 0.0) + jnp.log1p(jnp.exp(-jnp.abs(x)))


def _bdot(a, b):
    return jnp.dot(a.astype(BF16), b, preferred_element_type=F32)


def _split_dot(x, e):
    hi = x.astype(BF16)
    lo = (x - hi.astype(F32)).astype(BF16)
    return (jnp.dot(hi, e, preferred_element_type=F32)
            + jnp.dot(lo, e, preferred_element_type=F32))


def _head_sum(x, e_red, e_exp):
    return _split_dot(_split_dot(x, e_red), e_exp)


def _row_tile(n, pref):
    t = min(n, pref)
    while n % t:
        t //= 2
    return t


def _ffn_kernel(x_ref, g_ref, wg_ref, wu_ref, wd_ref, o_ref, h_ref):
    @pl.when(pl.program_id(1) == 0)
    def _():
        x = x_ref[...]
        h_ref[...] = _rms(x, g_ref[...]).astype(BF16)
        o_ref[...] = x

    h = h_ref[...]
    a = jnp.dot(h, wg_ref[...].astype(BF16), preferred_element_type=F32)
    b = jnp.dot(h, wu_ref[...].astype(BF16), preferred_element_type=F32)
    hid = (0.5 * a * jax.nn.sigmoid(a)) * b
    o_ref[...] += jnp.dot(hid.astype(BF16), wd_ref[...].astype(BF16), preferred_element_type=F32)


def _ffn(x, g, wg, wu, wd):
    n, d = x.shape
    f = wg.shape[1]
    tm = _row_tile(n, 1024)
    tf = _row_tile(f, 256)
    return pl.pallas_call(
        _ffn_kernel,
        grid=(n // tm, f // tf),
        in_specs=[
            pl.BlockSpec((tm, d), lambda i, j: (i, 0), pipeline_mode=pl.Buffered(1)),
            pl.BlockSpec((1, d), lambda i, j: (0, 0)),
            pl.BlockSpec((d, tf), lambda i, j: (0, j)),
            pl.BlockSpec((d, tf), lambda i, j: (0, j)),
            pl.BlockSpec((tf, d), lambda i, j: (j, 0)),
        ],
        out_specs=pl.BlockSpec((tm, d), lambda i, j: (i, 0)),
        out_shape=jax.ShapeDtypeStruct((n, d), F32),
        scratch_shapes=[pltpu.VMEM((tm, d), BF16)],
        compiler_params=_params("parallel", "arbitrary"),
        name="ffn",
    )(x, g, wg, wu, wd)


def _ple_kernel(x_ref, g_ref, wgate_ref, p_ref, wproj_ref, o_ref):
    x = x_ref[...]
    gate = jax.nn.sigmoid(_bdot(_rms(x, g_ref[...]), wgate_ref[...]))
    o_ref[...] = x + gate * _bdot(p_ref[...], wproj_ref[...])


def _ple(x, g, wgate, p, wproj):
    n, d = x.shape
    dp = p.shape[1]
    tm = _row_tile(n, 512)
    return pl.pallas_call(
        _ple_kernel,
        grid=(n // tm,),
        in_specs=[
            pl.BlockSpec((tm, d), lambda i: (i, 0)),
            pl.BlockSpec((1, d), lambda i: (0, 0)),
            pl.BlockSpec((d, d), lambda i: (0, 0)),
            pl.BlockSpec((tm, dp), lambda i: (i, 0)),
            pl.BlockSpec((dp, d), lambda i: (0, 0)),
        ],
        out_specs=pl.BlockSpec((tm, d), lambda i: (i, 0)),
        out_shape=jax.ShapeDtypeStruct((n, d), F32),
        compiler_params=_params("parallel"),
        name="ple",
    )(x, g, wgate, p, wproj)


def _rmsnorm_kernel(x_ref, g_ref, o_ref):
    o_ref[...] = _rms(x_ref[...], g_ref[...])


def _rmsnorm(x, g):
    n, d = x.shape
    tm = _row_tile(n, 512)
    return pl.pallas_call(
        _rmsnorm_kernel,
        grid=(n // tm,),
        in_specs=[pl.BlockSpec((tm, d), lambda i: (i, 0)),
                  pl.BlockSpec((1, d), lambda i: (0, 0))],
        out_specs=pl.BlockSpec((tm, d), lambda i: (i, 0)),
        out_shape=jax.ShapeDtypeStruct((n, d), F32),
        compiler_params=_params("parallel"),
        name="rmsnorm",
    )(x, g)


def _inproj_kernel(x_ref, g_ref, w_ref, o_ref, h_ref):
    @pl.when(pl.program_id(1) == 0)
    def _():
        h_ref[...] = _rms(x_ref[...], g_ref[...]).astype(BF16)

    o_ref[...] = jnp.dot(h_ref[...], w_ref[...], preferred_element_type=F32)


def _inproj(x, g, w):
    n, d = x.shape
    c = w.shape[1]
    tm = _row_tile(n, 512)
    tn = _row_tile(c, 512)
    return pl.pallas_call(
        _inproj_kernel,
        grid=(n // tm, c // tn),
        in_specs=[
            pl.BlockSpec((tm, d), lambda i, j: (i, 0)),
            pl.BlockSpec((1, d), lambda i, j: (0, 0)),
            pl.BlockSpec((d, tn), lambda i, j: (0, j)),
        ],
        out_specs=pl.BlockSpec((tm, tn), lambda i, j: (i, j)),
        out_shape=jax.ShapeDtypeStruct((n, c), F32),
        scratch_shapes=[pltpu.VMEM((tm, d), BF16)],
        compiler_params=_params("parallel", "arbitrary"),
        name="inproj",
    )(x, g, w)


def _rwkv_prep_kernel(z_ref, shift_ref, mu_ref, w0_ref, w2_ref, a0_ref, a2_ref, g2_ref,
                      kk_ref, ka_ref, ered_ref, eexp_ref,
                      r_ref, w_ref, k_ref, v_ref, nkk_ref, kka_ref, g_ref, carry_ref, *, wa):
    tb = pl.program_id(1)
    z = z_ref[...]
    bb, tt, ca = z.shape

    @pl.when(tb == 0)
    def _():
        carry_ref[...] = shift_ref[...]

    t_idx = lax.broadcasted_iota(jnp.int32, z.shape, 1)
    zprev = jnp.where(t_idx == 0, carry_ref[...], pltpu.roll(z, 1, axis=1))
    carry_ref[...] = z[:, tt - 1:tt, :]
    zs = (z + (zprev - z) * mu_ref[...]).reshape(bb * tt, ca)

    r = zs[:, 0:wa]
    k = zs[:, wa:2 * wa]
    v = zs[:, 2 * wa:3 * wa]
    xwa = zs[:, 3 * wa:3 * wa + LANES]
    xg = zs[:, 3 * wa + LANES:3 * wa + 3 * LANES]
    w_log = -_softplus(-(w0_ref[...] + _bdot(jnp.tanh(xwa), w2_ref[...]))) - 0.5
    decay = jnp.exp(-jnp.exp(w_log))
    a = jax.nn.sigmoid(a0_ref[...] + _bdot(xwa, a2_ref[...]))
    g = _bdot(jax.nn.sigmoid(xg), g2_ref[...])
    kk = k * kk_ref[...]
    kn = kk / jnp.maximum(jnp.sqrt(_head_sum(kk * kk, ered_ref[...], eexp_ref[...])), 1e-12)
    k2 = k * (1.0 + (a - 1.0) * ka_ref[...])
    shp = (bb, tt, wa)
    r_ref[...] = r.reshape(shp)
    w_ref[...] = decay.reshape(shp)
    k_ref[...] = k2.reshape(shp)
    v_ref[...] = v.reshape(shp)
    nkk_ref[...] = (-kn).reshape(shp)
    kka_ref[...] = (kn * a).reshape(shp)
    g_ref[...] = g.reshape(shp)


def _rwkv_prep(z3, shift, mu, w0, w2p, a0, a2p, g2p, kk, ka, ered, eexp, wa, ca, bb, tt):
    b, t, _ = z3.shape
    row = lambda i, j: (0, 0)
    vec = lambda c: pl.BlockSpec((1, c), row)
    out_spec = pl.BlockSpec((bb, tt, wa), lambda i, j: (i, j, 0))
    out_shape = jax.ShapeDtypeStruct((b, t, wa), F32)
    return pl.pallas_call(
        functools.partial(_rwkv_prep_kernel, wa=wa),
        grid=(b // bb, t // tt),
        in_specs=[
            pl.BlockSpec((bb, tt, ca), lambda i, j: (i, j, 0)),
            pl.BlockSpec((bb, 1, ca), lambda i, j: (i, 0, 0)),
            vec(ca), vec(wa),
            pl.BlockSpec(w2p.shape, row), vec(wa),
            pl.BlockSpec(a2p.shape, row), pl.BlockSpec(g2p.shape, row),
            vec(wa), vec(wa),
            pl.BlockSpec(ered.shape, row), pl.BlockSpec(eexp.shape, row),
        ],
        out_specs=[out_spec] * 7,
        out_shape=[out_shape] * 7,
        scratch_shapes=[pltpu.VMEM((bb, 1, ca), F32)],
        compiler_params=_params("parallel", "arbitrary"),
        name="rwkv_prep",
    )(z3, shift, mu, w0, w2p, a0, a2p, g2p, kk, ka, ered, eexp)


def _wkv_kernel(kv_ref, v_ref, s0_ref, y_ref, s_ref):
    tt = kv_ref.shape[1]
    n_key = kv_ref.shape[3]
    slab = s_ref.shape[2:]

    @pl.when(pl.program_id(1) == 0)
    def _():
        s_ref[...] = s0_ref[...]

    def tree_sum(parts):
        while len(parts) > 1:
            parts = [parts[i] + parts[i + 1] for i in range(0, len(parts), 2)]
        return parts[0]

    def step(t, carry):
        key_row = lambda c, j: kv_ref[0, t, c, j:j + 1, :]
        acc = [jnp.zeros(slab, F32) for _ in range(N_ACC)]
        for j in range(n_key):
            acc[j % N_ACC] = acc[j % N_ACC] + s_ref[0, j] * key_row(0, j)
        sa = tree_sum(acc)
        v = v_ref[0, t]
        acc = [jnp.zeros(slab, F32) for _ in range(N_ACC)]
        for j in range(n_key):
            s = s_ref[0, j] * key_row(1, j) + sa * key_row(2, j) + v * key_row(3, j)
            s_ref[0, j] = s
            acc[j % N_ACC] = acc[j % N_ACC] + s * key_row(4, j)
        y_ref[0, t] = tree_sum(acc)
        return carry

    lax.fori_loop(0, tt, step, 0)


def _wkv(kv, v, s0, tt):
    u, t, n_vec, n_key, _ = kv.shape
    rows = v.shape[2]
    st = pl.BlockSpec((1, n_key, rows, LANES), lambda i, j: (i, 0, 0, 0))
    seq = pl.BlockSpec((1, tt, rows, LANES), lambda i, j: (i, j, 0, 0))
    return pl.pallas_call(
        _wkv_kernel,
        grid=(u, t // tt),
        in_specs=[pl.BlockSpec((1, tt, n_vec, n_key, LANES), lambda i, j: (i, j, 0, 0, 0)), seq, st],
        out_specs=[seq, st],
        out_shape=[jax.ShapeDtypeStruct(v.shape, F32), jax.ShapeDtypeStruct(s0.shape, F32)],
        compiler_params=_params("parallel", "arbitrary"),
        name="wkv",
    )(kv, v, s0)


def _mix_out_kernel(x_ref, y_ref, r_ref, k_ref, v_ref, g_ref, yb_ref, rk_ref, lng_ref, lnb_ref,
                    ered_ref, eexp_ref, woa_ref, wob_ref, o_ref):
    ered = ered_ref[...]
    eexp = eexp_ref[...]
    y = y_ref[...]
    v = v_ref[...]
    mean = _head_sum(y, ered, eexp) * (1.0 / HEAD)
    d = y - mean
    var = _head_sum(d * d, ered, eexp) * (1.0 / HEAD)
    yn = d * lax.rsqrt(var + GN_EPS) * lng_ref[...] + lnb_ref[...]
    bonus = _head_sum(r_ref[...] * k_ref[...] * rk_ref[...], ered, eexp) * v
    ya = (yn + bonus) * g_ref[...]
    o_ref[...] = x_ref[...] + _bdot(ya, woa_ref[...]) + _bdot(yb_ref[...], wob_ref[...])


def _mix_out(x, y, r, k, v, g, yb, rk, lng, lnb, ered, eexp, woa, wob):
    n, d = x.shape
    wa = y.shape[1]
    wb = yb.shape[1]
    tm = _row_tile(n, 256)
    row = lambda i: (0, 0)
    ta = pl.BlockSpec((tm, wa), lambda i: (i, 0))
    va = pl.BlockSpec((1, wa), row)
    return pl.pallas_call(
        _mix_out_kernel,
        grid=(n // tm,),
        in_specs=[pl.BlockSpec((tm, d), lambda i: (i, 0)), ta, ta, ta, ta, ta,
                  pl.BlockSpec((tm, wb), lambda i: (i, 0)), va, va, va,
                  pl.BlockSpec(ered.shape, row), pl.BlockSpec(eexp.shape, row),
                  pl.BlockSpec(woa.shape, row), pl.BlockSpec(wob.shape, row)],
        out_specs=pl.BlockSpec((tm, d), lambda i: (i, 0)),
        out_shape=jax.ShapeDtypeStruct((n, d), F32),
        compiler_params=_params("parallel"),
        name="mix_out",
    )(x, y, r, k, v, g, yb, rk, lng, lnb, ered, eexp, woa, wob)


def _rglru_kernel(xb_ref, gb_ref, h0_ref, hist0_ref, cw_ref, cb_ref, wa_ref, ba_ref, wx_ref, bx_ref,
                  lam_ref, y_ref, hl_ref, hist_ref, h_ref, a_scr, b_scr):
    tb = pl.program_id(1)
    x = xb_ref[...]
    bb, tt, wb = x.shape
    n_tile = tt // SUBLANES

    @pl.when(tb == 0)
    def _():
        hist_ref[...] = hist0_ref[...]
        h_ref[...] = h0_ref[...]

    hist = hist_ref[...]
    hist_ref[...] = x[:, tt - SUBLANES:tt, :]
    cw = cw_ref[...]
    t8 = lax.broadcasted_iota(jnp.int32, (bb, SUBLANES, wb), 1)
    xc = cb_ref[...] + x * cw[CONV_W - 1:CONV_W, :]
    for dly in range(1, CONV_W):
        rolled = pltpu.roll(x, dly, axis=1)
        head = jnp.where(t8 < dly, pltpu.roll(hist, dly, axis=1), rolled[:, :SUBLANES, :])
        if n_tile > 1:
            shifted = jnp.concatenate([head, rolled[:, SUBLANES:, :]], axis=1)
        else:
            shifted = head
        xc = xc + shifted * cw[CONV_W - 1 - dly:CONV_W - dly, :]

    xc2 = xc.reshape(bb * tt, wb)
    xcb = xc2.astype(BF16)
    n_q = wa_ref.shape[0]
    wq = wb // n_q
    gr = jnp.concatenate(
        [jnp.dot(xcb[:, q * wq:(q + 1) * wq], wa_ref[q], preferred_element_type=F32) for q in range(n_q)],
        axis=1)
    gi = jnp.concatenate(
        [jnp.dot(xcb[:, q * wq:(q + 1) * wq], wx_ref[q], preferred_element_type=F32) for q in range(n_q)],
        axis=1)
    gate_r = jax.nn.sigmoid(gr + ba_ref[...])
    gate_i = jax.nn.sigmoid(gi + bx_ref[...])
    log_a = (-LRU_C) * gate_r * _softplus(-lam_ref[...])
    a = jnp.exp(log_a)
    b = jnp.sqrt(1.0 - jnp.exp(2.0 * log_a)) * (gate_i * xc2)
    a_scr[...] = a.reshape(bb, tt, wb)
    b_scr[...] = b.reshape(bb, tt, wb)

    def tile_scan(i, h):
        off = pl.multiple_of(i * SUBLANES, SUBLANES)
        at = a_scr[:, pl.ds(off, SUBLANES), :]
        bt = b_scr[:, pl.ds(off, SUBLANES), :]
        for dly in (1, 2, 4):
            keep = t8 >= dly
            bt = jnp.where(keep, at * pltpu.roll(bt, dly, axis=1) + bt, bt)
            at = jnp.where(keep, at * pltpu.roll(at, dly, axis=1), at)
        ht = bt + at * h
        b_scr[:, pl.ds(off, SUBLANES), :] = ht
        return ht[:, SUBLANES - 1:SUBLANES, :]

    h_last = lax.fori_loop(0, n_tile, tile_scan, h_ref[...])
    h_ref[...] = h_last
    hl_ref[...] = h_last
    y_ref[...] = b_scr[...] * jax.nn.gelu(gb_ref[...])


def _rglru(zb3, h0, hist0, cw, cb, wa4, ba, wx4, bx, lam, bb, tt):
    b, t, _ = zb3.shape
    wb = h0.shape[-1]
    row = lambda i, j: (0, 0)
    vec = pl.BlockSpec((1, wb), row)
    st = pl.BlockSpec((bb, 1, wb), lambda i, j: (i, 0, 0))
    return pl.pallas_call(
        _rglru_kernel,
        grid=(b // bb, t // tt),
        in_specs=[
            pl.BlockSpec((bb, tt, wb), lambda i, j: (i, j, 0)),
            pl.BlockSpec((bb, tt, wb), lambda i, j: (i, j, 1)),
            st,
            pl.BlockSpec((bb, SUBLANES, wb), lambda i, j: (i, 0, 0)),
            pl.BlockSpec((CONV_W, wb), row), vec,
            pl.BlockSpec(wa4.shape, lambda i, j: (0, 0, 0)), vec,
            pl.BlockSpec(wx4.shape, lambda i, j: (0, 0, 0)), vec, vec,
        ],
        out_specs=[pl.BlockSpec((bb, tt, wb), lambda i, j: (i, j, 0)), st],
        out_shape=[jax.ShapeDtypeStruct((b, t, wb), F32), jax.ShapeDtypeStruct((b, 1, wb), F32)],
        scratch_shapes=[pltpu.VMEM((bb, SUBLANES, wb), F32), pltpu.VMEM((bb, 1, wb), F32),
                        pltpu.VMEM((bb, tt, wb), F32), pltpu.VMEM((bb, tt, wb), F32)],
        compiler_params=_params("parallel", "arbitrary"),
        name="rglru",
    )(zb3, zb3, h0, hist0, cw, cb, wa4, ba, wx4, bx, lam)


def _s5_setup_kernel(are_ref, aim_ref, ldt_ref, bre_ref, bim_ref, cre_ref, cim_ref,
                     k_ref, mre_ref, mim_ref, pbre_ref, pbim_ref, pre_ref, pim_ref):
    a_re = are_ref[...]
    a_im = aim_ref[...]
    dt = jnp.exp(ldt_ref[...])
    mag = jnp.exp(dt * a_re)
    ab_re = mag * jnp.cos(dt * a_im)
    ab_im = mag * jnp.sin(dt * a_im)
    den = a_re * a_re + a_im * a_im
    f_re = ((ab_re - 1.0) * a_re + ab_im * a_im) / den
    f_im = (ab_im * a_re - (ab_re - 1.0) * a_im) / den
    b_re = bre_ref[...]
    b_im = bim_ref[...]
    bb_re = f_re[:, None, :] * b_re - f_im[:, None, :] * b_im
    bb_im = f_re[:, None, :] * b_im + f_im[:, None, :] * b_re
    c_re = cre_ref[...]
    c_im = cim_ref[...]
    rhs = jnp.concatenate([bb_re, bb_im], axis=2)
    p_re = jnp.ones_like(a_re)
    p_im = jnp.zeros_like(a_re)
    for tau in range(S5_CHUNK + 1):
        pre_ref[:, tau, :] = p_re
        pim_ref[:, tau, :] = p_im
        m_re = c_re * p_re[:, None, :] - c_im * p_im[:, None, :]
        m_im = c_re * p_im[:, None, :] + c_im * p_re[:, None, :]
        mre_ref[:, tau] = m_re
        mim_ref[:, tau] = m_im
        pbre_ref[:, tau] = p_re[:, None, :] * bb_re - p_im[:, None, :] * bb_im
        pbim_ref[:, tau] = p_re[:, None, :] * bb_im + p_im[:, None, :] * bb_re
        lhs = jnp.concatenate([m_re, -m_im], axis=2)
        k_ref[:, tau] = jnp.einsum("gmk,gnk->gmn", lhs, rhs, precision=lax.Precision.HIGHEST,
                                   preferred_element_type=F32)
        p_re, p_im = p_re * ab_re - p_im * ab_im, p_re * ab_im + p_im * ab_re


def _s5_setup(a_re, a_im, log_dt, b_re_t, b_im_t, c_re, c_im):
    g, p = a_re.shape
    c = c_re.shape[1]
    gb = 8
    nt = S5_CHUNK + 1
    g2 = pl.BlockSpec((gb, p), lambda i: (i, 0))
    g3 = pl.BlockSpec((gb, c, p), lambda i: (i, 0, 0))
    o4 = pl.BlockSpec((gb, nt, c, p), lambda i: (i, 0, 0, 0))
    o3 = pl.BlockSpec((gb, nt, p), lambda i: (i, 0, 0))
    s4 = jax.ShapeDtypeStruct((g, nt, c, p), F32)
    s3 = jax.ShapeDtypeStruct((g, nt, p), F32)
    return pl.pallas_call(
        _s5_setup_kernel,
        grid=(g // gb,),
        in_specs=[g2, g2, pl.BlockSpec((gb, 1), lambda i: (i, 0)), g3, g3, g3, g3],
        out_specs=[pl.BlockSpec((gb, nt, c, c), lambda i: (i, 0, 0, 0)), o4, o4, o4, o4, o3, o3],
        out_shape=[jax.ShapeDtypeStruct((g, nt, c, c), F32), s4, s4, s4, s4, s3, s3],
        compiler_params=_params("parallel"),
        name="s5_setup",
    )(a_re, a_im, log_dt, b_re_t, b_im_t, c_re, c_im)


def _s5_kernel(u_ref, h0r_ref, h0i_ref, bd_ref, gm_ref, hm_ref, ar_ref, ai_ref,
               y_ref, her_ref, hei_ref, kt_scr, gu_scr, hs_scr, *, chunk, n_chunk):
    sw = ar_ref.shape[-1]
    rows = u_ref.shape[0] // chunk

    @pl.when(pl.program_id(1) == 0)
    def _():
        kt_scr[...] = jnp.zeros_like(kt_scr)
        for s in range(chunk):
            for t in range(s, chunk):
                kt_scr[s * LANES:(s + 1) * LANES, t * LANES:(t + 1) * LANES] = bd_ref[0, t - s]

    ucat = jnp.concatenate(
        [u_ref[pl.ds(s, rows, stride=chunk), :] for s in range(chunk)], axis=1).astype(BF16)
    gu = jnp.dot(ucat, gm_ref[0], preferred_element_type=F32)
    a_r = ar_ref[0]
    a_i = ai_ref[0]

    def advance(h_r, h_i, g):
        return a_r * h_r - a_i * h_i + g[:, :sw], a_r * h_i + a_i * h_r + g[:, sw:]

    if n_chunk == 1:
        h_r = h0r_ref[...]
        h_i = h0i_ref[...]
        hs = jnp.concatenate([h_r, h_i], axis=1)
        e_r, e_i = advance(h_r, h_i, gu)
        her_ref[...] = e_r
        hei_ref[...] = e_i
    else:
        gu_scr[...] = gu
        group = SUBLANES if rows % SUBLANES == 0 else rows
        sub = lax.broadcasted_iota(jnp.int32, (group, 2 * sw), 0)

        def tile(it, h):
            h_r, h_i = h
            off = pl.multiple_of(it * group, group)
            g8 = gu_scr[pl.ds(off, group), :]
            hs8 = jnp.zeros((group, 2 * sw), F32)
            for i in range(group):
                hs8 = jnp.where(sub == i, jnp.concatenate([h_r, h_i], axis=1), hs8)
                h_r, h_i = advance(h_r, h_i, g8[i:i + 1, :])
            hs_scr[pl.ds(off, group), :] = hs8
            return h_r, h_i

        e_r, e_i = lax.fori_loop(0, rows // group, tile, (h0r_ref[0], h0i_ref[0]))
        her_ref[0] = e_r
        hei_ref[0] = e_i
        hs = hs_scr[...]
    y = (jnp.dot(ucat, kt_scr[...], preferred_element_type=F32)
         + jnp.dot(hs.astype(BF16), hm_ref[0], preferred_element_type=F32))
    for t in range(chunk):
        y_ref[pl.ds(t, rows, stride=chunk), :] = y[:, t * LANES:(t + 1) * LANES]


def _s5_conv(u, h0r, h0i, bd, gm, hm, ar, ai, b, t, chunk):
    n, d = u.shape
    n_unit = d // LANES
    sw = ar.shape[-1]
    n_chunk = t // chunk
    lw = chunk * LANES
    if n_chunk == 1:
        sb = b
        hspec = pl.BlockSpec((sb, sw), lambda q, i: (i, q))
        hshape = jax.ShapeDtypeStruct((b, n_unit * sw), F32)
    else:
        sb = 1
        h0r, h0i = h0r[:, None, :], h0i[:, None, :]
        hspec = pl.BlockSpec((1, 1, sw), lambda q, i: (i, 0, q))
        hshape = jax.ShapeDtypeStruct((b, 1, n_unit * sw), F32)
    rows = sb * n_chunk
    tok = pl.BlockSpec((sb * t, LANES), lambda q, i: (i, q))
    y, he_r, he_i = pl.pallas_call(
        functools.partial(_s5_kernel, chunk=chunk, n_chunk=n_chunk),
        grid=(n_unit, b // sb),
        in_specs=[
            tok, hspec, hspec,
            pl.BlockSpec((1, chunk, LANES, LANES), lambda q, i: (q, 0, 0, 0)),
            pl.BlockSpec((1, lw, 2 * sw), lambda q, i: (q, 0, 0)),
            pl.BlockSpec((1, 2 * sw, lw), lambda q, i: (q, 0, 0)),
            pl.BlockSpec((1, 1, sw), lambda q, i: (q, 0, 0)),
            pl.BlockSpec((1, 1, sw), lambda q, i: (q, 0, 0)),
        ],
        out_specs=[tok, hspec, hspec],
        out_shape=[jax.ShapeDtypeStruct((n, d), F32), hshape, hshape],
        scratch_shapes=[pltpu.VMEM((lw, lw), BF16), pltpu.VMEM((rows, 2 * sw), F32),
                        pltpu.VMEM((rows, 2 * sw), F32)],
        compiler_params=_params("arbitrary", "arbitrary"),
        name="s5_conv",
    )(u, h0r, h0i, bd, gm, hm, ar, ai)
    return y, he_r.reshape(b, -1), he_i.reshape(b, -1)


def _s5_out_kernel(x_ref, u_ref, yc_ref, d_ref, w_ref, b_ref, o_ref):
    z = jax.nn.gelu(yc_ref[...] + d_ref[...] * u_ref[...])
    o_ref[...] = x_ref[...] + z * jax.nn.sigmoid(_bdot(z, w_ref[...]) + b_ref[...])


def _s5_out(x, u, yc, d, w, b):
    n, dm = x.shape
    tm = _row_tile(n, 256)
    row = lambda i: (0, 0)
    tile = pl.BlockSpec((tm, dm), lambda i: (i, 0))
    vec = pl.BlockSpec((1, dm), row)
    return pl.pallas_call(
        _s5_out_kernel,
        grid=(n // tm,),
        in_specs=[tile, tile, tile, vec, pl.BlockSpec((dm, dm), row), vec],
        out_specs=tile,
        out_shape=jax.ShapeDtypeStruct((n, dm), F32),
        compiler_params=_params("parallel"),
        name="s5_out",
    )(x, u, yc, d, w, b)


def _block_diag_tiles(w, per_tile):
    n_blk, h, _ = w.shape
    n_tile = n_blk // per_tile
    w = w.reshape(n_tile, per_tile, h, h)
    eye = jnp.eye(per_tile, dtype=w.dtype)
    out = jnp.einsum("tphk,pq->tphqk", w, eye)
    return out.reshape(n_tile, per_tile * h, per_tile * h)


def _head_sum_mats(wa):
    onehot = (jnp.arange(wa)[:, None] // HEAD == jnp.arange(LANES)[None, :])
    return onehot.astype(BF16), onehot.T.astype(BF16)


def _s5_tables(k_tab, m_re, m_im, pb_re, pb_im, p_re, p_im, chunk):
    g, _, c, p = m_re.shape
    gu = LANES // c
    n_unit = g // gu
    eye = jnp.eye(gu, dtype=F32)
    unit = lambda a: a.reshape((n_unit, gu) + a.shape[1:])
    bd = jnp.einsum("ugtdc,gh->utgchd", unit(k_tab)[:, :, :chunk], eye).reshape(n_unit, chunk, LANES, LANES)
    rev = chunk - 1 - jnp.arange(chunk)
    gm = jnp.concatenate(
        [jnp.einsum("ugscp,gh->usgchp", unit(t)[:, :, rev], eye).reshape(n_unit, chunk * LANES, gu * p)
         for t in (pb_re, pb_im)], axis=2)
    hm = jnp.concatenate(
        [jnp.einsum("ugtdp,gh->ugpthd", unit(t)[:, :, 1:chunk + 1], eye).reshape(n_unit, gu * p, chunk * LANES)
         for t in (m_re, -m_im)], axis=1)
    ar = p_re[:, chunk].reshape(n_unit, 1, gu * p)
    ai = p_im[:, chunk].reshape(n_unit, 1, gu * p)
    return bd.astype(BF16), gm.astype(BF16), hm.astype(BF16), ar, ai


def _wkv_to_lanes(vecs, v, s0):
    b, t, wa = v.shape
    h = wa // HEAD
    nv = len(vecs)
    units = b * h
    if units <= LANES:
        rep = LANES // units
        ip = HEAD // rep
        kv = jnp.stack(vecs).reshape(nv, b, t, h, HEAD).transpose(2, 0, 4, 1, 3).reshape(t, nv, HEAD, 1, units)
        kv = jnp.broadcast_to(kv, (t, nv, HEAD, rep, units)).reshape(1, t, nv, HEAD, LANES)
        v2 = v.reshape(b, t, h, rep, ip).transpose(1, 4, 3, 0, 2).reshape(1, t, ip, LANES)
        s2 = s0.reshape(b, h, rep, ip, HEAD).transpose(4, 3, 2, 0, 1).reshape(1, HEAD, ip, LANES)
    else:
        nb = b // LANES
        kv = jnp.stack(vecs).reshape(nv, nb, LANES, t, h, HEAD).transpose(4, 1, 3, 0, 5, 2)
        kv = kv.reshape(h * nb, t, nv, HEAD, LANES)
        v2 = v.reshape(nb, LANES, t, h, HEAD).transpose(3, 0, 2, 4, 1).reshape(h * nb, t, HEAD, LANES)
        s2 = s0.reshape(nb, LANES, h, HEAD, HEAD).transpose(2, 0, 4, 3, 1).reshape(h * nb, HEAD, HEAD, LANES)
    return kv, v2, s2


def _wkv_from_lanes(y2, s2, b, t, h):
    units = b * h
    if units <= LANES:
        rep = LANES // units
        ip = HEAD // rep
        y = y2.reshape(t, ip, rep, b, h).transpose(3, 0, 4, 2, 1).reshape(b, t, h * HEAD)
        s = s2.reshape(HEAD, ip, rep, b, h).transpose(3, 4, 2, 1, 0).reshape(b, h, HEAD, HEAD)
    else:
        nb = b // LANES
        y = y2.reshape(h, nb, t, HEAD, LANES).transpose(1, 4, 2, 0, 3).reshape(b, t, h * HEAD)
        s = s2.reshape(h, nb, HEAD, HEAD, LANES).transpose(1, 4, 0, 3, 2).reshape(b, h, HEAD, HEAD)
    return y, s


def _seq_blocks(b, t):
    if t >= 256:
        return 1, 256
    return min(b, max(1, 256 // t)), t


def _prepare(w):
    depth = w["norm_ffn1"].shape[0]
    wa = w["w0_a"].shape[1]
    wb = w["lam_b"].shape[1]
    cols_a = w["mu_a"].shape[1]
    ca = ((cols_a + 511) // 512) * 512
    row = lambda v: v.reshape(1, -1)
    bf = lambda v: v.astype(BF16)
    pad_a = lambda v: jnp.pad(v, [(0, 0)] * (v.ndim - 1) + [(0, ca - cols_a)])
    layers = []
    for l in range(depth):
        j = l // 2
        q = dict(ple_gate=bf(w["ple_gate"][l]), ple_proj=bf(w["ple_proj"][l]))
        if l % 2 == 0:
            w_in = w["w_in_ab"][j]
            n_q = wb // (2 * LANES)
            w_out = bf(w["w_out_ab"][j])
            q.update(
                w_in_a=bf(pad_a(w_in[:, :cols_a])), w_in_b=bf(w_in[:, cols_a:]), mu=row(pad_a(w["mu_a"][j])),
                w2=bf(jnp.zeros((LANES, wa), F32).at[:LORA_W].set(w["w2_a"][j])),
                a2=bf(jnp.zeros((LANES, wa), F32).at[LORA_W:LORA_W + LORA_A].set(w["a2_a"][j])),
                g2=bf(jnp.zeros((2 * LANES, wa), F32).at[:LORA_G].set(w["g2_a"][j])),
                wa4=bf(_block_diag_tiles(w["wa_b"][j], (wb // HEAD) // n_q)),
                wx4=bf(_block_diag_tiles(w["wx_b"][j], (wb // HEAD) // n_q)),
                w_out_a=w_out[:wa], w_out_b=w_out[wa:])
        else:
            q.update(
                tabs=_s5_setup(w["a_re_c"][j], w["a_im_c"][j], w["log_dt_c"][j][:, None],
                               w["b_re_c"][j].transpose(0, 2, 1), w["b_im_c"][j].transpose(0, 2, 1),
                               w["c_re_c"][j], w["c_im_c"][j]),
                w_glu=bf(w["w_glu_c"][j]))
        layers.append(q)
    return layers


def _trunk(x, p, st_wkv, st_shift, st_h, st_conv, st_cre, st_cim, w, wp):
    b, t, d = x.shape
    n = b * t
    depth = w["norm_ffn1"].shape[0]
    wa = w["w0_a"].shape[1]
    wb = w["lam_b"].shape[1]
    cols_a = w["mu_a"].shape[1]
    ca = wp[0]["mu"].shape[1]
    n_head = wa // HEAD
    n_grp = w["a_re_c"].shape[1]
    chunk = min(S5_CHUNK, t)
    ered, eexp = _head_sum_mats(wa)
    row = lambda v: v.reshape(1, -1)

    x = x.reshape(n, d)
    new = {k: [] for k in ("wkv", "shift", "h", "conv", "cre", "cim")}
    for l in range(depth):
        j = l // 2
        q = wp[l]
        x = _ffn(x, row(w["norm_ffn1"][l]), w["ffn1_wg"][l], w["ffn1_wu"][l], w["ffn1_wd"][l])
        if l % 2 == 0:
            z3 = _inproj(x, row(w["norm_mix"][l]), q["w_in_a"]).reshape(b, t, ca)
            zb3 = _inproj(x, row(w["norm_mix"][l]), q["w_in_b"]).reshape(b, t, 2 * wb)
            shift = jnp.pad(st_shift[j], ((0, 0), (0, ca - cols_a)))[:, None, :]
            bb, tt = _seq_blocks(b, t)
            r, dec, k2, v, nkk, kka, g = _rwkv_prep(
                z3, shift, q["mu"], row(w["w0_a"][j]), q["w2"], row(w["a0_a"][j]), q["a2"], q["g2"],
                row(w["kk_a"][j]), row(w["ka_a"][j]), ered, eexp, wa, ca, bb, tt)
            kv, v2, s2 = _wkv_to_lanes([nkk, dec, kka, k2, r], v, st_wkv[j])
            y2, s2 = _wkv(kv, v2, s2, min(t, 32))
            y, s_new = _wkv_from_lanes(y2, s2, b, t, n_head)
            hist0 = jnp.pad(st_conv[j], ((0, 0), (SUBLANES - (CONV_W - 1), 0), (0, 0)))
            yb, h_new = _rglru(
                zb3, st_h[j][:, None, :], hist0, w["conv_w_b"][j], row(w["conv_b_b"][j]), q["wa4"],
                row(w["ba_b"][j]), q["wx4"], row(w["bx_b"][j]), row(w["lam_b"][j]), bb, tt)
            f2 = lambda a: a.reshape(n, -1)
            x = _mix_out(x, f2(y), f2(r), f2(k2), f2(v), f2(g), f2(yb), row(w["rk_a"][j]),
                         row(w["lnx_g"][j]), row(w["lnx_b"][j]), ered, eexp, q["w_out_a"], q["w_out_b"])
            new["wkv"].append(s_new)
            new["shift"].append(z3[:, t - 1, :cols_a])
            new["h"].append(h_new[:, 0, :])
            conv_all = jnp.concatenate([st_conv[j], zb3[:, max(t - (CONV_W - 1), 0):, :wb]], axis=1)
            new["conv"].append(conv_all[:, conv_all.shape[1] - (CONV_W - 1):])
        else:
            u = _rmsnorm(x, row(w["norm_mix"][l]))
            bd, gm, hm, ar, ai = _s5_tables(*q["tabs"], chunk)
            yc, he_r, he_i = _s5_conv(u, st_cre[j].reshape(b, -1), st_cim[j].reshape(b, -1),
                                      bd, gm, hm, ar, ai, b, t, chunk)
            x = _s5_out(x, u, yc, row(w["d_c"][j]), q["w_glu"], row(w["b_glu_c"][j]))
            new["cre"].append(he_r.reshape(b, n_grp, P_C))
            new["cim"].append(he_i.reshape(b, n_grp, P_C))
        x = _ffn(x, row(w["norm_ffn2"][l]), w["ffn2_wg"][l], w["ffn2_wu"][l], w["ffn2_wd"][l])
        x = _ple(x, row(w["norm_ple"][l]), q["ple_gate"], p[l].reshape(n, -1), q["ple_proj"])
    y = _rmsnorm(x, row(w["final_norm"])).reshape(b, t, d)
    stk = lambda name, ref: jnp.stack(new[name]).astype(ref.dtype)
    return y, (stk("wkv", st_wkv), stk("shift", st_shift), stk("h", st_h),
               stk("conv", st_conv), stk("cre", st_cre), stk("cim", st_cim))


def kernel(x_prompt, x_sample, state_a_wkv, state_a_shift, state_b_h, state_b_conv, state_c_re, state_c_im, p_prompt, p_sample, norm_ffn1, ffn1_wg, ffn1_wu, ffn1_wd, norm_mix, norm_ffn2, ffn2_wg, ffn2_wu, ffn2_wd, norm_ple, ple_gate, ple_proj, w_in_ab, mu_a, w0_a, w2_a, a0_a, a2_a, g2_a, kk_a, ka_a, rk_a, lnx_g, lnx_b, conv_w_b, conv_b_b, wa_b, ba_b, wx_b, bx_b, lam_b, w_out_ab, a_re_c, a_im_c, log_dt_c, b_re_c, b_im_c, c_re_c, c_im_c, d_c, w_glu_c, b_glu_c, final_norm):
    w = dict(norm_ffn1=norm_ffn1, ffn1_wg=ffn1_wg, ffn1_wu=ffn1_wu, ffn1_wd=ffn1_wd,
             norm_mix=norm_mix, norm_ffn2=norm_ffn2, ffn2_wg=ffn2_wg, ffn2_wu=ffn2_wu,
             ffn2_wd=ffn2_wd, norm_ple=norm_ple, ple_gate=ple_gate, ple_proj=ple_proj,
             w_in_ab=w_in_ab, mu_a=mu_a, w0_a=w0_a, w2_a=w2_a, a0_a=a0_a, a2_a=a2_a, g2_a=g2_a,
             kk_a=kk_a, ka_a=ka_a, rk_a=rk_a.reshape(rk_a.shape[0], -1), lnx_g=lnx_g, lnx_b=lnx_b,
             conv_w_b=conv_w_b, conv_b_b=conv_b_b, wa_b=wa_b, ba_b=ba_b, wx_b=wx_b, bx_b=bx_b,
             lam_b=lam_b, w_out_ab=w_out_ab,
             a_re_c=a_re_c, a_im_c=a_im_c, log_dt_c=log_dt_c, b_re_c=b_re_c, b_im_c=b_im_c,
             c_re_c=c_re_c, c_im_c=c_im_c, d_c=d_c, w_glu_c=w_glu_c, b_glu_c=b_glu_c,
             final_norm=final_norm)
    wp = _prepare(w)
    bp = x_prompt.shape[0]
    zeros = lambda s: jnp.zeros((s.shape[0], bp) + s.shape[2:], s.dtype)
    y_prompt, prompt_state = _trunk(
        x_prompt, p_prompt, zeros(state_a_wkv), zeros(state_a_shift), zeros(state_b_h),
        zeros(state_b_conv), zeros(state_c_re), zeros(state_c_im), w, wp)
    y_sample, sample_state = _trunk(
        x_sample, p_sample, state_a_wkv, state_a_shift, state_b_h, state_b_conv,
        state_c_re, state_c_im, w, wp)
    return (y_prompt, y_sample) + tuple(prompt_state) + tuple(sample_state)
```

```python
import functools

import jax
import jax.numpy as jnp
from jax import lax
from jax.experimental import pallas as pl
from jax.experimental.pallas import tpu as pltpu

F32 = jnp.float32
BF16 = jnp.bfloat16

HEAD = 64
LANES = 128
SUBLANES = 8
LORA_W = 64
LORA_A = 64
LORA_G = 160
CONV_W = 4
LRU_C = 8.0
GRP_C = 16
P_C = 64
RMS_EPS = 1e-6
GN_EPS = 64e-5
S5_CHUNK = 16
N_ACC = 4
VMEM_LIMIT = 56 * 1024 * 1024


def _params(*sem):
    return pltpu.CompilerParams(dimension_semantics=sem, vmem_limit_bytes=VMEM_LIMIT)


def _rms(x, g):
    return x * lax.rsqrt(jnp.mean(x * x, axis=-1, keepdims=True) + RMS_EPS) * g


def _softplus(x):
    return jnp.maximum(x, 0.0) + jnp.log1p(jnp.exp(-jnp.abs(x)))


def _bdot(a, b):
    return jnp.dot(a.astype(BF16), b, preferred_element_type=F32)


def _split_dot(x, e):
    hi = x.astype(BF16)
    lo = (x - hi.astype(F32)).astype(BF16)
    return (jnp.dot(hi, e, preferred_element_type=F32)
            + jnp.dot(lo, e, preferred_element_type=F32))


def _head_sum(x, e_red, e_exp):
    return _split_dot(_split_dot(x, e_red), e_exp)


def _row_tile(n, pref):
    t = min(n, pref)
    while n % t:
        t //= 2
    return t


def _ffn_kernel(x_ref, g_ref, wg_ref, wu_ref, wd_ref, o_ref, h_ref):
    @pl.when(pl.program_id(1) == 0)
    def _():
        x = x_ref[...]
        h_ref[...] = _rms(x, g_ref[...]).astype(BF16)
        o_ref[...] = x

    h = h_ref[...]
    a = jnp.dot(h, wg_ref[...].astype(BF16), preferred_element_type=F32)
    b = jnp.dot(h, wu_ref[...].astype(BF16), preferred_element_type=F32)
    hid = (0.5 * a * jax.nn.sigmoid(a)) * b
    o_ref[...] += jnp.dot(hid.astype(BF16), wd_ref[...].astype(BF16), preferred_element_type=F32)


def _ffn(x, g, wg, wu, wd, l):
    n, d = x.shape
    f = wg.shape[2]
    tm = _row_tile(n, 1024)
    tf = _row_tile(f, 256)
    return pl.pallas_call(
        _ffn_kernel,
        grid=(n // tm, f // tf),
        in_specs=[
            pl.BlockSpec((tm, d), lambda i, j: (i, 0), pipeline_mode=pl.Buffered(1)),
            pl.BlockSpec((1, d), lambda i, j: (0, 0)),
            pl.BlockSpec((None, d, tf), lambda i, j: (l, 0, j)),
            pl.BlockSpec((None, d, tf), lambda i, j: (l, 0, j)),
            pl.BlockSpec((None, tf, d), lambda i, j: (l, j, 0)),
        ],
        out_specs=pl.BlockSpec((tm, d), lambda i, j: (i, 0)),
        out_shape=jax.ShapeDtypeStruct((n, d), F32),
        scratch_shapes=[pltpu.VMEM((tm, d), BF16)],
        compiler_params=_params("parallel", "arbitrary"),
        name="ffn",
    )(x, g, wg, wu, wd)


def _ple_kernel(x_ref, g_ref, wgate_ref, p_ref, wproj_ref, o_ref):
    x = x_ref[...]
    gate = jax.nn.sigmoid(_bdot(_rms(x, g_ref[...]), wgate_ref[...]))
    o_ref[...] = x + gate * _bdot(p_ref[...], wproj_ref[...])


def _ple(x, g, wgate, p, wproj):
    n, d = x.shape
    dp = p.shape[1]
    tm = _row_tile(n, 512)
    return pl.pallas_call(
        _ple_kernel,
        grid=(n // tm,),
        in_specs=[
            pl.BlockSpec((tm, d), lambda i: (i, 0)),
            pl.BlockSpec((1, d), lambda i: (0, 0)),
            pl.BlockSpec((d, d), lambda i: (0, 0)),
            pl.BlockSpec((tm, dp), lambda i: (i, 0)),
            pl.BlockSpec((dp, d), lambda i: (0, 0)),
        ],
        out_specs=pl.BlockSpec((tm, d), lambda i: (i, 0)),
        out_shape=jax.ShapeDtypeStruct((n, d), F32),
        compiler_params=_params("parallel"),
        name="ple",
    )(x, g, wgate, p, wproj)


def _rmsnorm_kernel(x_ref, g_ref, o_ref):
    o_ref[...] = _rms(x_ref[...], g_ref[...])


def _rmsnorm(x, g):
    n, d = x.shape
    tm = _row_tile(n, 512)
    return pl.pallas_call(
        _rmsnorm_kernel,
        grid=(n // tm,),
        in_specs=[pl.BlockSpec((tm, d), lambda i: (i, 0)),
                  pl.BlockSpec((1, d), lambda i: (0, 0))],
        out_specs=pl.BlockSpec((tm, d), lambda i: (i, 0)),
        out_shape=jax.ShapeDtypeStruct((n, d), F32),
        compiler_params=_params("parallel"),
        name="rmsnorm",
    )(x, g)


def _inproj_kernel(x_ref, g_ref, wa_ref, wb_ref, za_ref, zb_ref):
    h = _rms(x_ref[...], g_ref[...]).astype(BF16)
    za_ref[...] = jnp.dot(h, wa_ref[...], preferred_element_type=F32)
    zb_ref[...] = jnp.dot(h, wb_ref[...], preferred_element_type=F32)


def _inproj(x, g, w_a, w_b):
    n, d = x.shape
    ca, cb = w_a.shape[1], w_b.shape[1]
    tm = _row_tile(n, 256)
    fixed = lambda shape: pl.BlockSpec(shape, lambda i: (0, 0), pipeline_mode=pl.Buffered(1))
    return pl.pallas_call(
        _inproj_kernel,
        grid=(n // tm,),
        in_specs=[pl.BlockSpec((tm, d), lambda i: (i, 0)), pl.BlockSpec((1, d), lambda i: (0, 0)),
                  fixed((d, ca)), fixed((d, cb))],
        out_specs=[pl.BlockSpec((tm, ca), lambda i: (i, 0)), pl.BlockSpec((tm, cb), lambda i: (i, 0))],
        out_shape=[jax.ShapeDtypeStruct((n, ca), F32), jax.ShapeDtypeStruct((n, cb), F32)],
        compiler_params=_params("parallel"),
        name="inproj",
    )(x, g, w_a, w_b)


def _rwkv_prep_kernel(z_ref, shift_ref, mu_ref, w0_ref, w2_ref, a0_ref, a2_ref, g2_ref,
                      kk_ref, ka_ref, ered_ref, eexp_ref,
                      r_ref, w_ref, k_ref, v_ref, nkk_ref, kka_ref, g_ref, carry_ref, *, wa):
    tb = pl.program_id(1)
    z = z_ref[...]
    bb, tt, ca = z.shape

    @pl.when(tb == 0)
    def _():
        carry_ref[...] = shift_ref[...]

    t_idx = lax.broadcasted_iota(jnp.int32, z.shape, 1)
    zprev = jnp.where(t_idx == 0, carry_ref[...], pltpu.roll(z, 1, axis=1))
    carry_ref[...] = z[:, tt - 1:tt, :]
    zs = (z + (zprev - z) * mu_ref[...]).reshape(bb * tt, ca)

    r = zs[:, 0:wa]
    k = zs[:, wa:2 * wa]
    v = zs[:, 2 * wa:3 * wa]
    xwa = zs[:, 3 * wa:3 * wa + LANES]
    xg = zs[:, 3 * wa + LANES:3 * wa + 3 * LANES]
    w_log = -_softplus(-(w0_ref[...] + _bdot(jnp.tanh(xwa), w2_ref[...]))) - 0.5
    decay = jnp.exp(-jnp.exp(w_log))
    a = jax.nn.sigmoid(a0_ref[...] + _bdot(xwa, a2_ref[...]))
    g = _bdot(jax.nn.sigmoid(xg), g2_ref[...])
    kk = k * kk_ref[...]
    kn = kk / jnp.maximum(jnp.sqrt(_head_sum(kk * kk, ered_ref[...], eexp_ref[...])), 1e-12)
    k2 = k * (1.0 + (a - 1.0) * ka_ref[...])
    shp = (bb, tt, wa)
    r_ref[...] = r.reshape(shp)
    w_ref[...] = decay.reshape(shp)
    k_ref[...] = k2.reshape(shp)
    v_ref[...] = v.reshape(shp)
    nkk_ref[...] = (-kn).reshape(shp)
    kka_ref[...] = (kn * a).reshape(shp)
    g_ref[...] = g.reshape(shp)


def _rwkv_prep(z3, shift, mu, w0, w2p, a0, a2p, g2p, kk, ka, ered, eexp, wa, ca, bb, tt):
    b, t, _ = z3.shape
    row = lambda i, j: (0, 0)
    vec = lambda c: pl.BlockSpec((1, c), row)
    out_spec = pl.BlockSpec((bb, tt, wa), lambda i, j: (i, j, 0))
    out_shape = jax.ShapeDtypeStruct((b, t, wa), F32)
    return pl.pallas_call(
        functools.partial(_rwkv_prep_kernel, wa=wa),
        grid=(b // bb, t // tt),
        in_specs=[
            pl.BlockSpec((bb, tt, ca), lambda i, j: (i, j, 0)),
            pl.BlockSpec((bb, 1, ca), lambda i, j: (i, 0, 0)),
            vec(ca), vec(wa),
            pl.BlockSpec(w2p.shape, row), vec(wa),
            pl.BlockSpec(a2p.shape, row), pl.BlockSpec(g2p.shape, row),
            vec(wa), vec(wa),
            pl.BlockSpec(ered.shape, row), pl.BlockSpec(eexp.shape, row),
        ],
        out_specs=[out_spec] * 7,
        out_shape=[out_shape] * 7,
        scratch_shapes=[pltpu.VMEM((bb, 1, ca), F32)],
        compiler_params=_params("parallel", "arbitrary"),
        name="rwkv_prep",
    )(z3, shift, mu, w0, w2p, a0, a2p, g2p, kk, ka, ered, eexp)


def _to_lanes_kernel(x_ref, o_ref, xt_scr, *, rep, split):
    bsz, tt, wa = x_ref.shape
    n_head = wa // HEAD
    n_rows = HEAD // rep if split else HEAD
    for b in range(bsz):
        xt_scr[b] = x_ref[b].T
    for r in range(n_rows):
        offs = [r + q * n_rows for q in range(rep)] if split else [r] * rep
        m = jnp.concatenate(
            [xt_scr[b, pl.ds(off, n_head, stride=HEAD), :] for off in offs for b in range(bsz)], axis=0).T
        if split:
            o_ref[pl.ds(r, tt, stride=n_rows), :] = m
        else:
            o_ref[r] = m.reshape(tt // SUBLANES, SUBLANES, LANES)


def _to_lanes(x, rep, split, tt):
    b, t, wa = x.shape
    n_rows = HEAD // rep if split else HEAD
    if split:
        out_spec = pl.BlockSpec((tt * n_rows, LANES), lambda i: (i, 0))
        out_shape = jax.ShapeDtypeStruct((t * n_rows, LANES), F32)
    else:
        out_spec = pl.BlockSpec((HEAD, tt // SUBLANES, SUBLANES, LANES), lambda i: (0, i, 0, 0))
        out_shape = jax.ShapeDtypeStruct((HEAD, t // SUBLANES, SUBLANES, LANES), F32)
    return pl.pallas_call(
        functools.partial(_to_lanes_kernel, rep=rep, split=split),
        grid=(t // tt,),
        in_specs=[pl.BlockSpec((b, tt, wa), lambda i: (0, i, 0))],
        out_specs=out_spec,
        out_shape=out_shape,
        scratch_shapes=[pltpu.VMEM((b, wa, tt), F32)],
        compiler_params=_params("parallel"),
        name="to_lanes",
    )(x)


def _from_lanes_kernel(y_ref, o_ref, yt_scr, *, rep):
    bsz, tt, wa = o_ref.shape
    n_head = wa // HEAD
    n_rows = HEAD // rep
    for r in range(n_rows):
        m = y_ref[pl.ds(r, tt, stride=n_rows), :].T
        for q in range(rep):
            for b in range(bsz):
                k0 = (q * bsz + b) * n_head
                yt_scr[b, pl.ds(q * n_rows + r, n_head, stride=HEAD), :] = m[k0:k0 + n_head, :]
    for b in range(bsz):
        o_ref[b] = yt_scr[b].T


def _from_lanes(y2, b, wa, rep, tt):
    n_rows = HEAD // rep
    t = y2.shape[0] // n_rows
    return pl.pallas_call(
        functools.partial(_from_lanes_kernel, rep=rep),
        grid=(t // tt,),
        in_specs=[pl.BlockSpec((tt * n_rows, LANES), lambda i: (i, 0))],
        out_specs=pl.BlockSpec((b, tt, wa), lambda i: (0, i, 0)),
        out_shape=jax.ShapeDtypeStruct((b, t, wa), F32),
        scratch_shapes=[pltpu.VMEM((b, wa, tt), F32)],
        compiler_params=_params("parallel"),
        name="from_lanes",
    )(y2)


def _wkv_kernel(nkk_ref, w_ref, kka_ref, k_ref, r_ref, v_ref, s0_ref, y_ref, s_ref):
    n_key, n_t8 = nkk_ref.shape[1], nkk_ref.shape[2]
    slab = s_ref.shape[2:]

    @pl.when(pl.program_id(1) == 0)
    def _():
        s_ref[...] = s0_ref[...]

    def tree_sum(parts):
        while len(parts) > 1:
            parts = [parts[i] + parts[i + 1] for i in range(0, len(parts), 2)]
        return parts[0]

    def tile_step(t8, carry):
        for i in range(SUBLANES):
            t = t8 * SUBLANES + i
            key_row = lambda ref, j: ref[0, j, t8, i:i + 1, :]
            acc = [jnp.zeros(slab, F32) for _ in range(N_ACC)]
            for j in range(n_key):
                acc[j % N_ACC] = acc[j % N_ACC] + s_ref[0, j] * key_row(nkk_ref, j)
            sa = tree_sum(acc)
            v = v_ref[0, t]
            acc = [jnp.zeros(slab, F32) for _ in range(N_ACC)]
            for j in range(n_key):
                s = s_ref[0, j] * key_row(w_ref, j) + sa * key_row(kka_ref, j) + v * key_row(k_ref, j)
                s_ref[0, j] = s
                acc[j % N_ACC] = acc[j % N_ACC] + s * key_row(r_ref, j)
            y_ref[0, t] = tree_sum(acc)
        return carry

    lax.fori_loop(0, n_t8, tile_step, 0)


def _wkv(keys, v, s0, tt):
    u, t, rows, _ = v.shape
    n_key = s0.shape[1]
    st = pl.BlockSpec((1, n_key, rows, LANES), lambda i, j: (i, 0, 0, 0))
    seq = pl.BlockSpec((1, tt, rows, LANES), lambda i, j: (i, j, 0, 0))
    key = pl.BlockSpec((1, n_key, tt // SUBLANES, SUBLANES, LANES), lambda i, j: (i, 0, j, 0, 0))
    return pl.pallas_call(
        _wkv_kernel,
        grid=(u, t // tt),
        in_specs=[key] * 5 + [seq, st],
        out_specs=[seq, st],
        out_shape=[jax.ShapeDtypeStruct(v.shape, F32), jax.ShapeDtypeStruct(s0.shape, F32)],
        compiler_params=_params("parallel", "arbitrary"),
        name="wkv",
    )(*keys, v, s0)


def _mix_out_kernel(x_ref, y_ref, r_ref, k_ref, v_ref, g_ref, yb_ref, rk_ref, lng_ref, lnb_ref,
                    ered_ref, eexp_ref, woa_ref, wob_ref, o_ref):
    ered = ered_ref[...]
    eexp = eexp_ref[...]
    y = y_ref[...]
    v = v_ref[...]
    mean = _head_sum(y, ered, eexp) * (1.0 / HEAD)
    d = y - mean
    var = _head_sum(d * d, ered, eexp) * (1.0 / HEAD)
    yn = d * lax.rsqrt(var + GN_EPS) * lng_ref[...] + lnb_ref[...]
    bonus = _head_sum(r_ref[...] * k_ref[...] * rk_ref[...], ered, eexp) * v
    ya = (yn + bonus) * g_ref[...]
    o_ref[...] = x_ref[...] + _bdot(ya, woa_ref[...]) + _bdot(yb_ref[...], wob_ref[...])


def _mix_out(x, y, r, k, v, g, yb, rk, lng, lnb, ered, eexp, woa, wob):
    n, d = x.shape
    wa = y.shape[1]
    wb = yb.shape[1]
    tm = _row_tile(n, 256)
    row = lambda i: (0, 0)
    ta = pl.BlockSpec((tm, wa), lambda i: (i, 0))
    va = pl.BlockSpec((1, wa), row)
    return pl.pallas_call(
        _mix_out_kernel,
        grid=(n // tm,),
        in_specs=[pl.BlockSpec((tm, d), lambda i: (i, 0)), ta, ta, ta, ta, ta,
                  pl.BlockSpec((tm, wb), lambda i: (i, 0)), va, va, va,
                  pl.BlockSpec(ered.shape, row), pl.BlockSpec(eexp.shape, row),
                  pl.BlockSpec(woa.shape, row), pl.BlockSpec(wob.shape, row)],
        out_specs=pl.BlockSpec((tm, d), lambda i: (i, 0)),
        out_shape=jax.ShapeDtypeStruct((n, d), F32),
        compiler_params=_params("parallel"),
        name="mix_out",
    )(x, y, r, k, v, g, yb, rk, lng, lnb, ered, eexp, woa, wob)


def _rglru_kernel(xb_ref, gb_ref, h0_ref, hist0_ref, cw_ref, cb_ref, wa_ref, ba_ref, wx_ref, bx_ref,
                  lam_ref, y_ref, hl_ref, hist_ref, h_ref, a_scr, b_scr):
    tb = pl.program_id(1)
    x = xb_ref[...]
    bb, tt, wb = x.shape
    n_tile = tt // SUBLANES

    @pl.when(tb == 0)
    def _():
        hist_ref[...] = hist0_ref[...]
        h_ref[...] = h0_ref[...]

    hist = hist_ref[...]
    hist_ref[...] = x[:, tt - SUBLANES:tt, :]
    cw = cw_ref[...]
    t8 = lax.broadcasted_iota(jnp.int32, (bb, SUBLANES, wb), 1)
    xc = cb_ref[...] + x * cw[CONV_W - 1:CONV_W, :]
    for dly in range(1, CONV_W):
        rolled = pltpu.roll(x, dly, axis=1)
        head = jnp.where(t8 < dly, pltpu.roll(hist, dly, axis=1), rolled[:, :SUBLANES, :])
        if n_tile > 1:
            shifted = jnp.concatenate([head, rolled[:, SUBLANES:, :]], axis=1)
        else:
            shifted = head
        xc = xc + shifted * cw[CONV_W - 1 - dly:CONV_W - dly, :]

    xc2 = xc.reshape(bb * tt, wb)
    xcb = xc2.astype(BF16)
    n_q = wa_ref.shape[0]
    wq = wb // n_q
    gr = jnp.concatenate(
        [jnp.dot(xcb[:, q * wq:(q + 1) * wq], wa_ref[q], preferred_element_type=F32) for q in range(n_q)],
        axis=1)
    gi = jnp.concatenate(
        [jnp.dot(xcb[:, q * wq:(q + 1) * wq], wx_ref[q], preferred_element_type=F32) for q in range(n_q)],
        axis=1)
    gate_r = jax.nn.sigmoid(gr + ba_ref[...])
    gate_i = jax.nn.sigmoid(gi + bx_ref[...])
    log_a = (-LRU_C) * gate_r * _softplus(-lam_ref[...])
    a = jnp.exp(log_a)
    b = jnp.sqrt(1.0 - jnp.exp(2.0 * log_a)) * (gate_i * xc2)
    a_scr[...] = a.reshape(bb, tt, wb)
    b_scr[...] = b.reshape(bb, tt, wb)

    def tile_scan(i, h):
        off = pl.multiple_of(i * SUBLANES, SUBLANES)
        at = a_scr[:, pl.ds(off, SUBLANES), :]
        bt = b_scr[:, pl.ds(off, SUBLANES), :]
        for dly in (1, 2, 4):
            keep = t8 >= dly
            bt = jnp.where(keep, at * pltpu.roll(bt, dly, axis=1) + bt, bt)
            at = jnp.where(keep, at * pltpu.roll(at, dly, axis=1), at)
        ht = bt + at * h
        b_scr[:, pl.ds(off, SUBLANES), :] = ht
        return ht[:, SUBLANES - 1:SUBLANES, :]

    h_last = lax.fori_loop(0, n_tile, tile_scan, h_ref[...])
    h_ref[...] = h_last
    hl_ref[...] = h_last
    y_ref[...] = b_scr[...] * jax.nn.gelu(gb_ref[...])


def _rglru(zb3, h0, hist0, cw, cb, wa4, ba, wx4, bx, lam, bb, tt):
    b, t, _ = zb3.shape
    wb = h0.shape[-1]
    row = lambda i, j: (0, 0)
    vec = pl.BlockSpec((1, wb), row)
    st = pl.BlockSpec((bb, 1, wb), lambda i, j: (i, 0, 0))
    return pl.pallas_call(
        _rglru_kernel,
        grid=(b // bb, t // tt),
        in_specs=[
            pl.BlockSpec((bb, tt, wb), lambda i, j: (i, j, 0)),
            pl.BlockSpec((bb, tt, wb), lambda i, j: (i, j, 1)),
            st,
            pl.BlockSpec((bb, SUBLANES, wb), lambda i, j: (i, 0, 0)),
            pl.BlockSpec((CONV_W, wb), row), vec,
            pl.BlockSpec(wa4.shape, lambda i, j: (0, 0, 0)), vec,
            pl.BlockSpec(wx4.shape, lambda i, j: (0, 0, 0)), vec, vec,
        ],
        out_specs=[pl.BlockSpec((bb, tt, wb), lambda i, j: (i, j, 0)), st],
        out_shape=[jax.ShapeDtypeStruct((b, t, wb), F32), jax.ShapeDtypeStruct((b, 1, wb), F32)],
        scratch_shapes=[pltpu.VMEM((bb, SUBLANES, wb), F32), pltpu.VMEM((bb, 1, wb), F32),
                        pltpu.VMEM((bb, tt, wb), F32), pltpu.VMEM((bb, tt, wb), F32)],
        compiler_params=_params("parallel", "arbitrary"),
        name="rglru",
    )(zb3, zb3, h0, hist0, cw, cb, wa4, ba, wx4, bx, lam)


def _s5_setup_kernel(are_ref, aim_ref, ldt_ref, bre_ref, bim_ref, cre_ref, cim_ref,
                     bd_ref, gm_ref, hm_ref, pre_ref, pim_ref):
    a_re = are_ref[...]
    a_im = aim_ref[...]
    dt = jnp.exp(ldt_ref[...])
    mag = jnp.exp(dt * a_re)
    ab_re = mag * jnp.cos(dt * a_im)
    ab_im = mag * jnp.sin(dt * a_im)
    den = a_re * a_re + a_im * a_im
    f_re = ((ab_re - 1.0) * a_re + ab_im * a_im) / den
    f_im = (ab_im * a_re - (ab_re - 1.0) * a_im) / den
    b_re = bre_ref[...]
    b_im = bim_ref[...]
    bb_re = f_re[:, None, :] * b_re - f_im[:, None, :] * b_im
    bb_im = f_re[:, None, :] * b_im + f_im[:, None, :] * b_re
    c_re = cre_ref[...]
    c_im = cim_ref[...]
    rhs = jnp.concatenate([bb_re, bb_im], axis=2)
    gb, n_c, n_p = c_re.shape
    gc = gb * n_c
    sw = gb * n_p
    ii = lambda shape, dim: lax.broadcasted_iota(jnp.int32, shape, dim)
    same = lambda shape, rdiv, ldiv: ii(shape, 0) // rdiv == ii(shape, 1) // ldiv
    tile_p = (ii((n_p, sw), 0) == ii((n_p, sw), 1) % n_p).astype(BF16)
    tile_c = (ii((n_c, gc), 0) == ii((n_c, gc), 1) % n_c).astype(BF16)
    wide = lambda x, tile: jnp.dot(x.astype(BF16), tile, preferred_element_type=F32)
    lane_grp = ii((n_p, gc), 1) // n_c
    p_re = jnp.ones_like(a_re)
    p_im = jnp.zeros_like(a_re)
    for tau in range(S5_CHUNK + 1):
        pre_ref[:, tau, :] = p_re
        pim_ref[:, tau, :] = p_im
        m_re = c_re * p_re[:, None, :] - c_im * p_im[:, None, :]
        m_im = c_re * p_im[:, None, :] + c_im * p_re[:, None, :]
        if tau < S5_CHUNK:
            s = S5_CHUNK - 1 - tau
            pb_re = p_re[:, None, :] * bb_re - p_im[:, None, :] * bb_im
            pb_im = p_re[:, None, :] * bb_im + p_im[:, None, :] * bb_re
            for ri, pb in enumerate((pb_re, pb_im)):
                blk = jnp.where(same((gc, sw), n_c, n_p), wide(pb.reshape(gc, n_p), tile_p), 0.0)
                gm_ref[0, s * gc:(s + 1) * gc, ri * sw:(ri + 1) * sw] = blk.astype(BF16)
            lhs = jnp.concatenate([m_re, -m_im], axis=2)
            k = jnp.einsum("gmk,gnk->gmn", lhs, rhs, precision=lax.Precision.HIGHEST,
                           preferred_element_type=F32)
            kw = jnp.where(same((gc, gc), n_c, n_c), wide(k.reshape(gc, n_c), tile_c), 0.0)
            bd_ref[0, tau] = kw.T.astype(BF16)
        if tau >= 1:
            for ri, m in enumerate((m_re, -m_im)):
                m2 = m.reshape(gc, n_p).astype(BF16).astype(F32)
                mt = jnp.concatenate([m2, jnp.zeros_like(m2)], axis=1).T[:n_p]
                for g in range(gb):
                    hm_ref[0, ri * sw + g * n_p:ri * sw + (g + 1) * n_p, (tau - 1) * gc:tau * gc] = (
                        jnp.where(lane_grp == g, mt, 0.0).astype(BF16))
        p_re, p_im = p_re * ab_re - p_im * ab_im, p_re * ab_im + p_im * ab_re


def _s5_setup(a_re, a_im, log_dt, b_re_t, b_im_t, c_re, c_im):
    g, p = a_re.shape
    c = c_re.shape[1]
    gb = LANES // c
    n_unit = g // gb
    nt = S5_CHUNK + 1
    lw = S5_CHUNK * LANES
    sw = gb * p
    g2 = pl.BlockSpec((gb, p), lambda i: (i, 0))
    g3 = pl.BlockSpec((gb, c, p), lambda i: (i, 0, 0))
    o3 = pl.BlockSpec((gb, nt, p), lambda i: (i, 0, 0))
    s3 = jax.ShapeDtypeStruct((g, nt, p), F32)
    return pl.pallas_call(
        _s5_setup_kernel,
        grid=(n_unit,),
        in_specs=[g2, g2, pl.BlockSpec((gb, 1), lambda i: (i, 0)), g3, g3, g3, g3],
        out_specs=[pl.BlockSpec((1, S5_CHUNK, LANES, LANES), lambda i: (i, 0, 0, 0)),
                   pl.BlockSpec((1, lw, 2 * sw), lambda i: (i, 0, 0)),
                   pl.BlockSpec((1, 2 * sw, lw), lambda i: (i, 0, 0)), o3, o3],
        out_shape=[jax.ShapeDtypeStruct((n_unit, S5_CHUNK, LANES, LANES), BF16),
                   jax.ShapeDtypeStruct((n_unit, lw, 2 * sw), BF16),
                   jax.ShapeDtypeStruct((n_unit, 2 * sw, lw), BF16), s3, s3],
        compiler_params=_params("parallel"),
        name="s5_setup",
    )(a_re, a_im, log_dt, b_re_t, b_im_t, c_re, c_im)


def _s5_kernel(u_ref, h0r_ref, h0i_ref, bd_ref, gm_ref, hm_ref, ar_ref, ai_ref,
               y_ref, her_ref, hei_ref, kt_scr, hs_scr, *, chunk, n_chunk):
    sw = ar_ref.shape[-1]
    rows = u_ref.shape[0] // chunk

    kt_scr[...] = jnp.zeros_like(kt_scr)
    for s in range(chunk):
        for t in range(s, chunk):
            kt_scr[s * LANES:(s + 1) * LANES, t * LANES:(t + 1) * LANES] = bd_ref[0, t - s]

    ucat = jnp.concatenate(
        [u_ref[pl.ds(s, rows, stride=chunk), :] for s in range(chunk)], axis=1).astype(BF16)
    gu = jnp.dot(ucat, gm_ref[0], preferred_element_type=F32)
    a_r = ar_ref[0]
    a_i = ai_ref[0]

    def advance(h_r, h_i, g):
        return a_r * h_r - a_i * h_i + g[:, :sw], a_r * h_i + a_i * h_r + g[:, sw:]

    if n_chunk == 1:
        h_r = h0r_ref[...]
        h_i = h0i_ref[...]
        hs = jnp.concatenate([h_r, h_i], axis=1)
        e_r, e_i = advance(h_r, h_i, gu)
        her_ref[...] = e_r
        hei_ref[...] = e_i
    else:
        hs_scr[...] = gu
        group = SUBLANES if n_chunk % SUBLANES == 0 else n_chunk
        sub = lax.broadcasted_iota(jnp.int32, (group, 2 * sw), 0)
        for b in range(rows // n_chunk):
            def tile(it, h):
                h_r, h_i = h
                off = pl.multiple_of(b * n_chunk + it * group, group)
                g8 = hs_scr[pl.ds(off, group), :]
                hs8 = jnp.zeros((group, 2 * sw), F32)
                for i in range(group):
                    hs8 = jnp.where(sub == i, jnp.concatenate([h_r, h_i], axis=1), hs8)
                    h_r, h_i = advance(h_r, h_i, g8[i:i + 1, :])
                hs_scr[pl.ds(off, group), :] = hs8
                return h_r, h_i

            e_r, e_i = lax.fori_loop(0, n_chunk // group, tile, (h0r_ref[b], h0i_ref[b]))
            her_ref[b] = e_r
            hei_ref[b] = e_i
        hs = hs_scr[...]
    y = (jnp.dot(ucat, kt_scr[...], preferred_element_type=F32)
         + jnp.dot(hs.astype(BF16), hm_ref[0], preferred_element_type=F32))
    for t in range(chunk):
        y_ref[pl.ds(t, rows, stride=chunk), :] = y[:, t * LANES:(t + 1) * LANES]


def _s5_conv(u, h0r, h0i, bd, gm, hm, ar, ai, b, t, chunk):
    n, d = u.shape
    n_unit = d // LANES
    sw = ar.shape[-1]
    n_chunk = t // chunk
    lw = chunk * LANES
    if n_chunk == 1:
        hspec = pl.BlockSpec((b, sw), lambda q: (0, q))
        hshape = jax.ShapeDtypeStruct((b, n_unit * sw), F32)
    else:
        h0r, h0i = h0r[:, None, :], h0i[:, None, :]
        hspec = pl.BlockSpec((b, 1, sw), lambda q: (0, 0, q))
        hshape = jax.ShapeDtypeStruct((b, 1, n_unit * sw), F32)
    rows = b * n_chunk
    tok = pl.BlockSpec((n, LANES), lambda q: (0, q))
    y, he_r, he_i = pl.pallas_call(
        functools.partial(_s5_kernel, chunk=chunk, n_chunk=n_chunk),
        grid=(n_unit,),
        in_specs=[
            pl.BlockSpec((n, LANES), lambda q: (0, q), pipeline_mode=pl.Buffered(1)), hspec, hspec,
            pl.BlockSpec((1, chunk, LANES, LANES), lambda q: (q, 0, 0, 0)),
            pl.BlockSpec((1, lw, 2 * sw), lambda q: (q, S5_CHUNK // chunk - 1, 0)),
            pl.BlockSpec((1, 2 * sw, lw), lambda q: (q, 0, 0)),
            pl.BlockSpec((1, 1, sw), lambda q: (q, 0, 0)),
            pl.BlockSpec((1, 1, sw), lambda q: (q, 0, 0)),
        ],
        out_specs=[tok, hspec, hspec],
        out_shape=[jax.ShapeDtypeStruct((n, d), F32), hshape, hshape],
        scratch_shapes=[pltpu.VMEM((lw, lw), BF16), pltpu.VMEM((rows, 2 * sw), F32)],
        compiler_params=_params("parallel"),
        name="s5_conv",
    )(u, h0r, h0i, bd, gm, hm, ar, ai)
    return y, he_r.reshape(b, -1), he_i.reshape(b, -1)


def _s5_out_kernel(x_ref, u_ref, yc_ref, d_ref, w_ref, b_ref, o_ref):
    z = jax.nn.gelu(yc_ref[...] + d_ref[...] * u_ref[...])
    o_ref[...] = x_ref[...] + z * jax.nn.sigmoid(_bdot(z, w_ref[...]) + b_ref[...])


def _s5_out(x, u, yc, d, w, b):
    n, dm = x.shape
    tm = _row_tile(n, 256)
    row = lambda i: (0, 0)
    tile = pl.BlockSpec((tm, dm), lambda i: (i, 0))
    vec = pl.BlockSpec((1, dm), row)
    return pl.pallas_call(
        _s5_out_kernel,
        grid=(n // tm,),
        in_specs=[tile, tile, tile, vec, pl.BlockSpec((dm, dm), row), vec],
        out_specs=tile,
        out_shape=jax.ShapeDtypeStruct((n, dm), F32),
        compiler_params=_params("parallel"),
        name="s5_out",
    )(x, u, yc, d, w, b)


def _block_diag_tiles(w, per_tile):
    n_blk, h, _ = w.shape
    n_tile = n_blk // per_tile
    w = w.reshape(n_tile, per_tile, h, h)
    eye = jnp.eye(per_tile, dtype=w.dtype)
    out = jnp.einsum("tphk,pq->tphqk", w, eye)
    return out.reshape(n_tile, per_tile * h, per_tile * h)


def _head_sum_mats(wa):
    onehot = (jnp.arange(wa)[:, None] // HEAD == jnp.arange(LANES)[None, :])
    return onehot.astype(BF16), onehot.T.astype(BF16)


def _wkv_on_lanes(vecs, v, s0):
    b, t, wa = v.shape
    h = wa // HEAD
    t8 = t // SUBLANES
    tt = min(t, 32)
    if b * h <= LANES:
        rep = LANES // (b * h)
        ip = HEAD // rep
        tp = min(t, LANES)
        keys = [_to_lanes(x, rep, False, tp)[None] for x in vecs]
        v2 = _to_lanes(v, rep, True, tp).reshape(1, t, ip, LANES)
        s2 = s0.reshape(b, h, rep, ip, HEAD).transpose(4, 3, 2, 0, 1).reshape(1, HEAD, ip, LANES)
        y2, s2 = _wkv(keys, v2, s2, tt)
        y = _from_lanes(y2.reshape(t * ip, LANES), b, wa, rep, tp)
        s = s2.reshape(HEAD, ip, rep, b, h).transpose(3, 4, 2, 1, 0).reshape(b, h, HEAD, HEAD)
    else:
        nb = b // LANES
        lay = lambda x: x.reshape(nb, LANES, t8, SUBLANES, h, HEAD).transpose(4, 0, 5, 2, 3, 1).reshape(
            h * nb, HEAD, t8, SUBLANES, LANES)
        v2 = v.reshape(nb, LANES, t, h, HEAD).transpose(3, 0, 2, 4, 1).reshape(h * nb, t, HEAD, LANES)
        s2 = s0.reshape(nb, LANES, h, HEAD, HEAD).transpose(2, 0, 4, 3, 1).reshape(h * nb, HEAD, HEAD, LANES)
        y2, s2 = _wkv([lay(x) for x in vecs], v2, s2, tt)
        y = y2.reshape(h, nb, t, HEAD, LANES).transpose(1, 4, 2, 0, 3).reshape(b, t, wa)
        s = s2.reshape(h, nb, HEAD, HEAD, LANES).transpose(1, 4, 0, 3, 2).reshape(b, h, HEAD, HEAD)
    return y, s


def _seq_blocks(b, t):
    if t >= 256:
        return 1, 256
    return min(b, max(1, 256 // t)), t


def _prepare(w):
    depth = w["norm_ffn1"].shape[0]
    wa = w["w0_a"].shape[1]
    wb = w["lam_b"].shape[1]
    cols_a = w["mu_a"].shape[1]
    ca = ((cols_a + 511) // 512) * 512
    row = lambda v: v.reshape(1, -1)
    bf = lambda v: v.astype(BF16)
    pad_a = lambda v: jnp.pad(v, [(0, 0)] * (v.ndim - 1) + [(0, ca - cols_a)])
    layers = []
    for l in range(depth):
        j = l // 2
        q = dict(ple_gate=bf(w["ple_gate"][l]), ple_proj=bf(w["ple_proj"][l]))
        if l % 2 == 0:
            w_in = w["w_in_ab"][j]
            n_q = wb // (2 * LANES)
            w_out = bf(w["w_out_ab"][j])
            q.update(
                w_in_a=bf(pad_a(w_in[:, :cols_a])), w_in_b=bf(w_in[:, cols_a:]), mu=row(pad_a(w["mu_a"][j])),
                w2=bf(jnp.zeros((LANES, wa), F32).at[:LORA_W].set(w["w2_a"][j])),
                a2=bf(jnp.zeros((LANES, wa), F32).at[LORA_W:LORA_W + LORA_A].set(w["a2_a"][j])),
                g2=bf(jnp.zeros((2 * LANES, wa), F32).at[:LORA_G].set(w["g2_a"][j])),
                wa4=bf(_block_diag_tiles(w["wa_b"][j], (wb // HEAD) // n_q)),
                wx4=bf(_block_diag_tiles(w["wx_b"][j], (wb // HEAD) // n_q)),
                w_out_a=w_out[:wa], w_out_b=w_out[wa:])
        else:
            q.update(
                tabs=_s5_setup(w["a_re_c"][j], w["a_im_c"][j], w["log_dt_c"][j][:, None],
                               w["b_re_c"][j].transpose(0, 2, 1), w["b_im_c"][j].transpose(0, 2, 1),
                               w["c_re_c"][j], w["c_im_c"][j]),
                w_glu=bf(w["w_glu_c"][j]))
        layers.append(q)
    return layers


def _trunk(x, p, st_wkv, st_shift, st_h, st_conv, st_cre, st_cim, w, wp):
    b, t, d = x.shape
    n = b * t
    depth = w["norm_ffn1"].shape[0]
    wa = w["w0_a"].shape[1]
    wb = w["lam_b"].shape[1]
    cols_a = w["mu_a"].shape[1]
    ca = wp[0]["mu"].shape[1]
    n_head = wa // HEAD
    n_grp = w["a_re_c"].shape[1]
    chunk = min(S5_CHUNK, t)
    ered, eexp = _head_sum_mats(wa)
    row = lambda v: v.reshape(1, -1)

    x = x.reshape(n, d)
    new = {k: [] for k in ("wkv", "shift", "h", "conv", "cre", "cim")}
    for l in range(depth):
        j = l // 2
        q = wp[l]
        x = _ffn(x, row(w["norm_ffn1"][l]), w["ffn1_wg"], w["ffn1_wu"], w["ffn1_wd"], l)
        if l % 2 == 0:
            za, zb = _inproj(x, row(w["norm_mix"][l]), q["w_in_a"], q["w_in_b"])
            z3 = za.reshape(b, t, ca)
            zb3 = zb.reshape(b, t, 2 * wb)
            shift = jnp.pad(st_shift[j], ((0, 0), (0, ca - cols_a)))[:, None, :]
            bb, tt = _seq_blocks(b, t)
            r, dec, k2, v, nkk, kka, g = _rwkv_prep(
                z3, shift, q["mu"], row(w["w0_a"][j]), q["w2"], row(w["a0_a"][j]), q["a2"], q["g2"],
                row(w["kk_a"][j]), row(w["ka_a"][j]), ered, eexp, wa, ca, bb, tt)
            y, s_new = _wkv_on_lanes([nkk, dec, kka, k2, r], v, st_wkv[j])
            hist0 = jnp.pad(st_conv[j], ((0, 0), (SUBLANES - (CONV_W - 1), 0), (0, 0)))
            yb, h_new = _rglru(
                zb3, st_h[j][:, None, :], hist0, w["conv_w_b"][j], row(w["conv_b_b"][j]), q["wa4"],
                row(w["ba_b"][j]), q["wx4"], row(w["bx_b"][j]), row(w["lam_b"][j]), bb, tt)
            f2 = lambda a: a.reshape(n, -1)
            x = _mix_out(x, f2(y), f2(r), f2(k2), f2(v), f2(g), f2(yb), row(w["rk_a"][j]),
                         row(w["lnx_g"][j]), row(w["lnx_b"][j]), ered, eexp, q["w_out_a"], q["w_out_b"])
            new["wkv"].append(s_new)
            new["shift"].append(z3[:, t - 1, :cols_a])
            new["h"].append(h_new[:, 0, :])
            conv_all = jnp.concatenate([st_conv[j], zb3[:, max(t - (CONV_W - 1), 0):, :wb]], axis=1)
            new["conv"].append(conv_all[:, conv_all.shape[1] - (CONV_W - 1):])
        else:
            u = _rmsnorm(x, row(w["norm_mix"][l]))
            bd, gm, hm, p_re, p_im = q["tabs"]
            ar = p_re[:, chunk].reshape(bd.shape[0], 1, -1)
            ai = p_im[:, chunk].reshape(bd.shape[0], 1, -1)
            yc, he_r, he_i = _s5_conv(u, st_cre[j].reshape(b, -1), st_cim[j].reshape(b, -1),
                                      bd, gm, hm, ar, ai, b, t, chunk)
            x = _s5_out(x, u, yc, row(w["d_c"][j]), q["w_glu"], row(w["b_glu_c"][j]))
            new["cre"].append(he_r.reshape(b, n_grp, P_C))
            new["cim"].append(he_i.reshape(b, n_grp, P_C))
        x = _ffn(x, row(w["norm_ffn2"][l]), w["ffn2_wg"], w["ffn2_wu"], w["ffn2_wd"], l)
        x = _ple(x, row(w["norm_ple"][l]), q["ple_gate"], p[l].reshape(n, -1), q["ple_proj"])
    y = _rmsnorm(x, row(w["final_norm"])).reshape(b, t, d)
    stk = lambda name, ref: jnp.stack(new[name]).astype(ref.dtype)
    return y, (stk("wkv", st_wkv), stk("shift", st_shift), stk("h", st_h),
               stk("conv", st_conv), stk("cre", st_cre), stk("cim", st_cim))


def kernel(x_prompt, x_sample, state_a_wkv, state_a_shift, state_b_h, state_b_conv, state_c_re, state_c_im, p_prompt, p_sample, norm_ffn1, ffn1_wg, ffn1_wu, ffn1_wd, norm_mix, norm_ffn2, ffn2_wg, ffn2_wu, ffn2_wd, norm_ple, ple_gate, ple_proj, w_in_ab, mu_a, w0_a, w2_a, a0_a, a2_a, g2_a, kk_a, ka_a, rk_a, lnx_g, lnx_b, conv_w_b, conv_b_b, wa_b, ba_b, wx_b, bx_b, lam_b, w_out_ab, a_re_c, a_im_c, log_dt_c, b_re_c, b_im_c, c_re_c, c_im_c, d_c, w_glu_c, b_glu_c, final_norm):
    w = dict(norm_ffn1=norm_ffn1, ffn1_wg=ffn1_wg, ffn1_wu=ffn1_wu, ffn1_wd=ffn1_wd,
             norm_mix=norm_mix, norm_ffn2=norm_ffn2, ffn2_wg=ffn2_wg, ffn2_wu=ffn2_wu,
             ffn2_wd=ffn2_wd, norm_ple=norm_ple, ple_gate=ple_gate, ple_proj=ple_proj,
             w_in_ab=w_in_ab, mu_a=mu_a, w0_a=w0_a, w2_a=w2_a, a0_a=a0_a, a2_a=a2_a, g2_a=g2_a,
             kk_a=kk_a, ka_a=ka_a, rk_a=rk_a.reshape(rk_a.shape[0], -1), lnx_g=lnx_g, lnx_b=lnx_b,
             conv_w_b=conv_w_b, conv_b_b=conv_b_b, wa_b=wa_b, ba_b=ba_b, wx_b=wx_b, bx_b=bx_b,
             lam_b=lam_b, w_out_ab=w_out_ab,
             a_re_c=a_re_c, a_im_c=a_im_c, log_dt_c=log_dt_c, b_re_c=b_re_c, b_im_c=b_im_c,
             c_re_c=c_re_c, c_im_c=c_im_c, d_c=d_c, w_glu_c=w_glu_c, b_glu_c=b_glu_c,
             final_norm=final_norm)
    wp = _prepare(w)
    bp = x_prompt.shape[0]
    zeros = lambda s: jnp.zeros((s.shape[0], bp) + s.shape[2:], s.dtype)
    y_prompt, prompt_state = _trunk(
        x_prompt, p_prompt, zeros(state_a_wkv), zeros(state_a_shift), zeros(state_b_h),
        zeros(state_b_conv), zeros(state_c_re), zeros(state_c_im), w, wp)
    y_sample, sample_state = _trunk(
        x_sample, p_sample, state_a_wkv, state_a_shift, state_b_h, state_b_conv,
        state_c_re, state_c_im, w, wp)
    return (y_prompt, y_sample) + tuple(prompt_state) + tuple(sample_state)
```

```python
import functools

import jax
import jax.numpy as jnp
from jax import lax
from jax.experimental import pallas as pl
from jax.experimental.pallas import tpu as pltpu

F32 = jnp.float32
BF16 = jnp.bfloat16

HEAD = 64
LANES = 128
SUBLANES = 8
LORA_W = 64
LORA_A = 64
LORA_G = 160
CONV_W = 4
LRU_C = 8.0
GRP_C = 16
P_C = 64
RMS_EPS = 1e-6
GN_EPS = 64e-5
S5_CHUNK = 16
N_ACC = 4
VMEM_LIMIT = 56 * 1024 * 1024


def _params(*sem):
    return pltpu.CompilerParams(dimension_semantics=sem, vmem_limit_bytes=VMEM_LIMIT)


def _rms(x, g):
    return x * lax.rsqrt(jnp.mean(x * x, axis=-1, keepdims=True) + RMS_EPS) * g


def _softplus(x):
    return jnp.maximum(x, 0.0) + jnp.log1p(jnp.exp(-jnp.abs(x)))


def _bdot(a, b):
    return jnp.dot(a.astype(BF16), b, preferred_element_type=F32)


def _split_dot(x, e):
    hi = x.astype(BF16)
    lo = (x - hi.astype(F32)).astype(BF16)
    return (jnp.dot(hi, e, preferred_element_type=F32)
            + jnp.dot(lo, e, preferred_element_type=F32))


def _head_sum(x, e_red, e_exp):
    return _split_dot(_split_dot(x, e_red), e_exp)


def _row_tile(n, pref):
    t = min(n, pref)
    while n % t:
        t //= 2
    return t


def _ffn_kernel(x_ref, g_ref, wg_ref, wu_ref, wd_ref, o_ref, h_ref):
    @pl.when(pl.program_id(1) == 0)
    def _():
        x = x_ref[...]
        h_ref[...] = _rms(x, g_ref[...]).astype(BF16)
        o_ref[...] = x

    h = h_ref[...]
    a = jnp.dot(h, wg_ref[...].astype(BF16), preferred_element_type=F32)
    b = jnp.dot(h, wu_ref[...].astype(BF16), preferred_element_type=F32)
    hid = (0.5 * a * jax.nn.sigmoid(a)) * b
    o_ref[...] += jnp.dot(hid.astype(BF16), wd_ref[...].astype(BF16), preferred_element_type=F32)


def _ffn(x, g, wg, wu, wd, l):
    n, d = x.shape
    f = wg.shape[2]
    tm = _row_tile(n, 1024)
    tf = _row_tile(f, 256)
    return pl.pallas_call(
        _ffn_kernel,
        grid=(n // tm, f // tf),
        in_specs=[
            pl.BlockSpec((tm, d), lambda i, j: (i, 0), pipeline_mode=pl.Buffered(1)),
            pl.BlockSpec((1, d), lambda i, j: (0, 0)),
            pl.BlockSpec((None, d, tf), lambda i, j: (l, 0, j)),
            pl.BlockSpec((None, d, tf), lambda i, j: (l, 0, j)),
            pl.BlockSpec((None, tf, d), lambda i, j: (l, j, 0)),
        ],
        out_specs=pl.BlockSpec((tm, d), lambda i, j: (i, 0)),
        out_shape=jax.ShapeDtypeStruct((n, d), F32),
        scratch_shapes=[pltpu.VMEM((tm, d), BF16)],
        compiler_params=_params("parallel", "arbitrary"),
        name="ffn",
    )(x, g, wg, wu, wd)


def _ple_kernel(x_ref, g_ref, wgate_ref, p_ref, wproj_ref, *rest):
    o_ref = rest[-1]
    x = x_ref[...]
    gate = jax.nn.sigmoid(_bdot(_rms(x, g_ref[...]), wgate_ref[...]))
    y = x + gate * _bdot(p_ref[...], wproj_ref[...])
    o_ref[...] = _rms(y, rest[0][...]) if len(rest) == 2 else y


def _ple(x, g, wgate, p, wproj, final_g=None):
    n, d = x.shape
    dp = p.shape[1]
    tm = _row_tile(n, 512)
    vec = pl.BlockSpec((1, d), lambda i: (0, 0))
    extra = [] if final_g is None else [final_g]
    return pl.pallas_call(
        _ple_kernel,
        grid=(n // tm,),
        in_specs=[
            pl.BlockSpec((tm, d), lambda i: (i, 0)),
            vec,
            pl.BlockSpec((d, d), lambda i: (0, 0)),
            pl.BlockSpec((tm, dp), lambda i: (i, 0)),
            pl.BlockSpec((dp, d), lambda i: (0, 0)),
        ] + [vec] * len(extra),
        out_specs=pl.BlockSpec((tm, d), lambda i: (i, 0)),
        out_shape=jax.ShapeDtypeStruct((n, d), F32),
        compiler_params=_params("parallel"),
        name="ple",
    )(x, g, wgate, p, wproj, *extra)


def _rmsnorm_kernel(x_ref, g_ref, o_ref):
    o_ref[...] = _rms(x_ref[...], g_ref[...])


def _rmsnorm(x, g):
    n, d = x.shape
    tm = _row_tile(n, 512)
    return pl.pallas_call(
        _rmsnorm_kernel,
        grid=(n // tm,),
        in_specs=[pl.BlockSpec((tm, d), lambda i: (i, 0)),
                  pl.BlockSpec((1, d), lambda i: (0, 0))],
        out_specs=pl.BlockSpec((tm, d), lambda i: (i, 0)),
        out_shape=jax.ShapeDtypeStruct((n, d), F32),
        compiler_params=_params("parallel"),
        name="rmsnorm",
    )(x, g)


def _inproj_kernel(x_ref, g_ref, wa_ref, wb_ref, za_ref, zb_ref):
    h = _rms(x_ref[...], g_ref[...]).astype(BF16)
    za_ref[...] = jnp.dot(h, wa_ref[...], preferred_element_type=F32)
    zb_ref[...] = jnp.dot(h, wb_ref[...], preferred_element_type=F32)


def _inproj(x, g, w_a, w_b):
    n, d = x.shape
    ca, cb = w_a.shape[1], w_b.shape[1]
    tm = _row_tile(n, 256)
    fixed = lambda shape: pl.BlockSpec(shape, lambda i: (0, 0), pipeline_mode=pl.Buffered(1))
    return pl.pallas_call(
        _inproj_kernel,
        grid=(n // tm,),
        in_specs=[pl.BlockSpec((tm, d), lambda i: (i, 0)), pl.BlockSpec((1, d), lambda i: (0, 0)),
                  fixed((d, ca)), fixed((d, cb))],
        out_specs=[pl.BlockSpec((tm, ca), lambda i: (i, 0)), pl.BlockSpec((tm, cb), lambda i: (i, 0))],
        out_shape=[jax.ShapeDtypeStruct((n, ca), F32), jax.ShapeDtypeStruct((n, cb), F32)],
        compiler_params=_params("parallel"),
        name="inproj",
    )(x, g, w_a, w_b)


def _rwkv_prep_kernel(z_ref, shift_ref, mu_ref, w0_ref, w2_ref, a0_ref, a2_ref, g2_ref,
                      kk_ref, ka_ref, ered_ref, eexp_ref,
                      r_ref, w_ref, k_ref, v_ref, nkk_ref, kka_ref, g_ref, carry_ref, *, wa):
    tb = pl.program_id(1)
    z = z_ref[...]
    bb, tt, ca = z.shape

    @pl.when(tb == 0)
    def _():
        carry_ref[...] = shift_ref[...]

    t_idx = lax.broadcasted_iota(jnp.int32, z.shape, 1)
    zprev = jnp.where(t_idx == 0, carry_ref[...], pltpu.roll(z, 1, axis=1))
    carry_ref[...] = z[:, tt - 1:tt, :]
    zs = (z + (zprev - z) * mu_ref[...]).reshape(bb * tt, ca)

    r = zs[:, 0:wa]
    k = zs[:, wa:2 * wa]
    v = zs[:, 2 * wa:3 * wa]
    xwa = zs[:, 3 * wa:3 * wa + LANES]
    xg = zs[:, 3 * wa + LANES:3 * wa + 3 * LANES]
    w_log = -_softplus(-(w0_ref[...] + _bdot(jnp.tanh(xwa), w2_ref[...]))) - 0.5
    decay = jnp.exp(-jnp.exp(w_log))
    a = jax.nn.sigmoid(a0_ref[...] + _bdot(xwa, a2_ref[...]))
    g = _bdot(jax.nn.sigmoid(xg), g2_ref[...])
    kk = k * kk_ref[...]
    kn = kk / jnp.maximum(jnp.sqrt(_head_sum(kk * kk, ered_ref[...], eexp_ref[...])), 1e-12)
    k2 = k * (1.0 + (a - 1.0) * ka_ref[...])
    shp = (bb, tt, wa)
    r_ref[...] = r.reshape(shp)
    w_ref[...] = decay.reshape(shp)
    k_ref[...] = k2.reshape(shp)
    v_ref[...] = v.reshape(shp)
    nkk_ref[...] = (-kn).reshape(shp)
    kka_ref[...] = (kn * a).reshape(shp)
    g_ref[...] = g.reshape(shp)


def _rwkv_prep(z3, shift, mu, w0, w2p, a0, a2p, g2p, kk, ka, ered, eexp, wa, ca, bb, tt):
    b, t, _ = z3.shape
    row = lambda i, j: (0, 0)
    vec = lambda c: pl.BlockSpec((1, c), row)
    out_spec = pl.BlockSpec((bb, tt, wa), lambda i, j: (i, j, 0))
    out_shape = jax.ShapeDtypeStruct((b, t, wa), F32)
    return pl.pallas_call(
        functools.partial(_rwkv_prep_kernel, wa=wa),
        grid=(b // bb, t // tt),
        in_specs=[
            pl.BlockSpec((bb, tt, ca), lambda i, j: (i, j, 0)),
            pl.BlockSpec((bb, 1, ca), lambda i, j: (i, 0, 0)),
            vec(ca), vec(wa),
            pl.BlockSpec(w2p.shape, row), vec(wa),
            pl.BlockSpec(a2p.shape, row), pl.BlockSpec(g2p.shape, row),
            vec(wa), vec(wa),
            pl.BlockSpec(ered.shape, row), pl.BlockSpec(eexp.shape, row),
        ],
        out_specs=[out_spec] * 7,
        out_shape=[out_shape] * 7,
        scratch_shapes=[pltpu.VMEM((bb, 1, ca), F32)],
        compiler_params=_params("parallel", "arbitrary"),
        name="rwkv_prep",
    )(z3, shift, mu, w0, w2p, a0, a2p, g2p, kk, ka, ered, eexp)


def _to_lanes_kernel(x_ref, o_ref, xt_scr, *, rep, split):
    bsz, tt, wa = x_ref.shape
    n_head = wa // HEAD
    n_rows = HEAD // rep if split else HEAD
    for b in range(bsz):
        xt_scr[b] = x_ref[b].T
    for r in range(n_rows):
        offs = [r + q * n_rows for q in range(rep)] if split else [r] * rep
        m = jnp.concatenate(
            [xt_scr[b, pl.ds(off, n_head, stride=HEAD), :] for off in offs for b in range(bsz)], axis=0).T
        if split:
            o_ref[pl.ds(r, tt, stride=n_rows), :] = m
        else:
            o_ref[r] = m.reshape(tt // SUBLANES, SUBLANES, LANES)


def _to_lanes(x, rep, split, tt):
    b, t, wa = x.shape
    n_rows = HEAD // rep if split else HEAD
    if split:
        out_spec = pl.BlockSpec((tt * n_rows, LANES), lambda i: (i, 0))
        out_shape = jax.ShapeDtypeStruct((t * n_rows, LANES), F32)
    else:
        out_spec = pl.BlockSpec((HEAD, tt // SUBLANES, SUBLANES, LANES), lambda i: (0, i, 0, 0))
        out_shape = jax.ShapeDtypeStruct((HEAD, t // SUBLANES, SUBLANES, LANES), F32)
    return pl.pallas_call(
        functools.partial(_to_lanes_kernel, rep=rep, split=split),
        grid=(t // tt,),
        in_specs=[pl.BlockSpec((b, tt, wa), lambda i: (0, i, 0))],
        out_specs=out_spec,
        out_shape=out_shape,
        scratch_shapes=[pltpu.VMEM((b, wa, tt), F32)],
        compiler_params=_params("parallel"),
        name="to_lanes",
    )(x)


def _from_lanes_kernel(y_ref, o_ref, yt_scr, *, rep):
    bsz, tt, wa = o_ref.shape
    n_head = wa // HEAD
    n_rows = HEAD // rep
    for r in range(n_rows):
        m = y_ref[pl.ds(r, tt, stride=n_rows), :].T
        for q in range(rep):
            for b in range(bsz):
                k0 = (q * bsz + b) * n_head
                yt_scr[b, pl.ds(q * n_rows + r, n_head, stride=HEAD), :] = m[k0:k0 + n_head, :]
    for b in range(bsz):
        o_ref[b] = yt_scr[b].T


def _from_lanes(y2, b, wa, rep, tt):
    n_rows = HEAD // rep
    t = y2.shape[0] // n_rows
    return pl.pallas_call(
        functools.partial(_from_lanes_kernel, rep=rep),
        grid=(t // tt,),
        in_specs=[pl.BlockSpec((tt * n_rows, LANES), lambda i: (i, 0))],
        out_specs=pl.BlockSpec((b, tt, wa), lambda i: (0, i, 0)),
        out_shape=jax.ShapeDtypeStruct((b, t, wa), F32),
        scratch_shapes=[pltpu.VMEM((b, wa, tt), F32)],
        compiler_params=_params("parallel"),
        name="from_lanes",
    )(y2)


def _wkv_kernel(nkk_ref, w_ref, kka_ref, k_ref, r_ref, v_ref, s0_ref, y_ref, s_ref):
    n_key, n_t8 = nkk_ref.shape[1], nkk_ref.shape[2]
    slab = s_ref.shape[2:]

    @pl.when(pl.program_id(1) == 0)
    def _():
        s_ref[...] = s0_ref[...]

    def tree_sum(parts):
        while len(parts) > 1:
            parts = [parts[i] + parts[i + 1] for i in range(0, len(parts), 2)]
        return parts[0]

    def tile_step(t8, carry):
        for i in range(SUBLANES):
            t = t8 * SUBLANES + i
            key_row = lambda ref, j: ref[0, j, t8, i:i + 1, :]
            acc = [jnp.zeros(slab, F32) for _ in range(N_ACC)]
            for j in range(n_key):
                acc[j % N_ACC] = acc[j % N_ACC] + s_ref[0, j] * key_row(nkk_ref, j)
            sa = tree_sum(acc)
            v = v_ref[0, t]
            acc = [jnp.zeros(slab, F32) for _ in range(N_ACC)]
            for j in range(n_key):
                s = s_ref[0, j] * key_row(w_ref, j) + sa * key_row(kka_ref, j) + v * key_row(k_ref, j)
                s_ref[0, j] = s
                acc[j % N_ACC] = acc[j % N_ACC] + s * key_row(r_ref, j)
            y_ref[0, t] = tree_sum(acc)
        return carry

    lax.fori_loop(0, n_t8, tile_step, 0)


def _wkv(keys, v, s0, tt):
    u, t, rows, _ = v.shape
    n_key = s0.shape[1]
    st = pl.BlockSpec((1, n_key, rows, LANES), lambda i, j: (i, 0, 0, 0))
    seq = pl.BlockSpec((1, tt, rows, LANES), lambda i, j: (i, j, 0, 0))
    key = pl.BlockSpec((1, n_key, tt // SUBLANES, SUBLANES, LANES), lambda i, j: (i, 0, j, 0, 0))
    return pl.pallas_call(
        _wkv_kernel,
        grid=(u, t // tt),
        in_specs=[key] * 5 + [seq, st],
        out_specs=[seq, st],
        out_shape=[jax.ShapeDtypeStruct(v.shape, F32), jax.ShapeDtypeStruct(s0.shape, F32)],
        compiler_params=_params("parallel", "arbitrary"),
        name="wkv",
    )(*keys, v, s0)


def _mix_out_kernel(x_ref, y_ref, r_ref, k_ref, v_ref, g_ref, yb_ref, rk_ref, lng_ref, lnb_ref,
                    ered_ref, eexp_ref, woa_ref, wob_ref, o_ref):
    ered = ered_ref[...]
    eexp = eexp_ref[...]
    y = y_ref[...]
    v = v_ref[...]
    mean = _head_sum(y, ered, eexp) * (1.0 / HEAD)
    d = y - mean
    var = _head_sum(d * d, ered, eexp) * (1.0 / HEAD)
    yn = d * lax.rsqrt(var + GN_EPS) * lng_ref[...] + lnb_ref[...]
    bonus = _head_sum(r_ref[...] * k_ref[...] * rk_ref[...], ered, eexp) * v
    ya = (yn + bonus) * g_ref[...]
    o_ref[...] = x_ref[...] + _bdot(ya, woa_ref[...]) + _bdot(yb_ref[...], wob_ref[...])


def _mix_out(x, y, r, k, v, g, yb, rk, lng, lnb, ered, eexp, woa, wob):
    n, d = x.shape
    wa = y.shape[1]
    wb = yb.shape[1]
    tm = _row_tile(n, 256)
    row = lambda i: (0, 0)
    ta = pl.BlockSpec((tm, wa), lambda i: (i, 0))
    va = pl.BlockSpec((1, wa), row)
    return pl.pallas_call(
        _mix_out_kernel,
        grid=(n // tm,),
        in_specs=[pl.BlockSpec((tm, d), lambda i: (i, 0)), ta, ta, ta, ta, ta,
                  pl.BlockSpec((tm, wb), lambda i: (i, 0)), va, va, va,
                  pl.BlockSpec(ered.shape, row), pl.BlockSpec(eexp.shape, row),
                  pl.BlockSpec(woa.shape, row), pl.BlockSpec(wob.shape, row)],
        out_specs=pl.BlockSpec((tm, d), lambda i: (i, 0)),
        out_shape=jax.ShapeDtypeStruct((n, d), F32),
        compiler_params=_params("parallel"),
        name="mix_out",
    )(x, y, r, k, v, g, yb, rk, lng, lnb, ered, eexp, woa, wob)


def _rglru_kernel(xb_ref, gb_ref, h0_ref, hist0_ref, cw_ref, cb_ref, wa_ref, ba_ref, wx_ref, bx_ref,
                  lam_ref, y_ref, hl_ref, hist_ref, h_ref, a_scr, b_scr):
    tb = pl.program_id(1)
    x = xb_ref[...]
    bb, tt, wb = x.shape
    n_tile = tt // SUBLANES

    @pl.when(tb == 0)
    def _():
        hist_ref[...] = hist0_ref[...]
        h_ref[...] = h0_ref[...]

    hist = hist_ref[...]
    hist_ref[...] = x[:, tt - SUBLANES:tt, :]
    cw = cw_ref[...]
    t8 = lax.broadcasted_iota(jnp.int32, (bb, SUBLANES, wb), 1)
    xc = cb_ref[...] + x * cw[CONV_W - 1:CONV_W, :]
    for dly in range(1, CONV_W):
        rolled = pltpu.roll(x, dly, axis=1)
        head = jnp.where(t8 < dly, pltpu.roll(hist, dly, axis=1), rolled[:, :SUBLANES, :])
        if n_tile > 1:
            shifted = jnp.concatenate([head, rolled[:, SUBLANES:, :]], axis=1)
        else:
            shifted = head
        xc = xc + shifted * cw[CONV_W - 1 - dly:CONV_W - dly, :]

    xc2 = xc.reshape(bb * tt, wb)
    xcb = xc2.astype(BF16)
    n_q = wa_ref.shape[0]
    wq = wb // n_q
    gr = jnp.concatenate(
        [jnp.dot(xcb[:, q * wq:(q + 1) * wq], wa_ref[q], preferred_element_type=F32) for q in range(n_q)],
        axis=1)
    gi = jnp.concatenate(
        [jnp.dot(xcb[:, q * wq:(q + 1) * wq], wx_ref[q], preferred_element_type=F32) for q in range(n_q)],
        axis=1)
    gate_r = jax.nn.sigmoid(gr + ba_ref[...])
    gate_i = jax.nn.sigmoid(gi + bx_ref[...])
    log_a = (-LRU_C) * gate_r * _softplus(-lam_ref[...])
    a = jnp.exp(log_a)
    b = jnp.sqrt(1.0 - jnp.exp(2.0 * log_a)) * (gate_i * xc2)
    a_scr[...] = a.reshape(bb, tt, wb)
    b_scr[...] = b.reshape(bb, tt, wb)

    def tile_scan(i, h):
        off = pl.multiple_of(i * SUBLANES, SUBLANES)
        at = a_scr[:, pl.ds(off, SUBLANES), :]
        bt = b_scr[:, pl.ds(off, SUBLANES), :]
        for dly in (1, 2, 4):
            keep = t8 >= dly
            bt = jnp.where(keep, at * pltpu.roll(bt, dly, axis=1) + bt, bt)
            at = jnp.where(keep, at * pltpu.roll(at, dly, axis=1), at)
        ht = bt + at * h
        b_scr[:, pl.ds(off, SUBLANES), :] = ht
        return ht[:, SUBLANES - 1:SUBLANES, :]

    h_last = lax.fori_loop(0, n_tile, tile_scan, h_ref[...])
    h_ref[...] = h_last
    hl_ref[...] = h_last
    y_ref[...] = b_scr[...] * jax.nn.gelu(gb_ref[...])


def _rglru(zb3, h0, hist0, cw, cb, wa4, ba, wx4, bx, lam, bb, tt):
    b, t, _ = zb3.shape
    wb = h0.shape[-1]
    row = lambda i, j: (0, 0)
    vec = pl.BlockSpec((1, wb), row)
    st = pl.BlockSpec((bb, 1, wb), lambda i, j: (i, 0, 0))
    return pl.pallas_call(
        _rglru_kernel,
        grid=(b // bb, t // tt),
        in_specs=[
            pl.BlockSpec((bb, tt, wb), lambda i, j: (i, j, 0)),
            pl.BlockSpec((bb, tt, wb), lambda i, j: (i, j, 1)),
            st,
            pl.BlockSpec((bb, SUBLANES, wb), lambda i, j: (i, 0, 0)),
            pl.BlockSpec((CONV_W, wb), row), vec,
            pl.BlockSpec(wa4.shape, lambda i, j: (0, 0, 0)), vec,
            pl.BlockSpec(wx4.shape, lambda i, j: (0, 0, 0)), vec, vec,
        ],
        out_specs=[pl.BlockSpec((bb, tt, wb), lambda i, j: (i, j, 0)), st],
        out_shape=[jax.ShapeDtypeStruct((b, t, wb), F32), jax.ShapeDtypeStruct((b, 1, wb), F32)],
        scratch_shapes=[pltpu.VMEM((bb, SUBLANES, wb), F32), pltpu.VMEM((bb, 1, wb), F32),
                        pltpu.VMEM((bb, tt, wb), F32), pltpu.VMEM((bb, tt, wb), F32)],
        compiler_params=_params("parallel", "arbitrary"),
        name="rglru",
    )(zb3, zb3, h0, hist0, cw, cb, wa4, ba, wx4, bx, lam)


def _s5_setup_kernel(are_ref, aim_ref, ldt_ref, bre_ref, bim_ref, cre_ref, cim_ref,
                     bd_ref, gm_ref, hm_ref, pre_ref, pim_ref):
    a_re = are_ref[...]
    a_im = aim_ref[...]
    dt = jnp.exp(ldt_ref[...])
    mag = jnp.exp(dt * a_re)
    ab_re = mag * jnp.cos(dt * a_im)
    ab_im = mag * jnp.sin(dt * a_im)
    den = a_re * a_re + a_im * a_im
    f_re = ((ab_re - 1.0) * a_re + ab_im * a_im) / den
    f_im = (ab_im * a_re - (ab_re - 1.0) * a_im) / den
    b_re = bre_ref[...]
    b_im = bim_ref[...]
    bb_re = f_re[:, None, :] * b_re - f_im[:, None, :] * b_im
    bb_im = f_re[:, None, :] * b_im + f_im[:, None, :] * b_re
    c_re = cre_ref[...]
    c_im = cim_ref[...]
    rhs = jnp.concatenate([bb_re, bb_im], axis=2)
    gb, n_c, n_p = c_re.shape
    gc = gb * n_c
    sw = gb * n_p
    ii = lambda shape, dim: lax.broadcasted_iota(jnp.int32, shape, dim)
    same = lambda shape, rdiv, ldiv: ii(shape, 0) // rdiv == ii(shape, 1) // ldiv
    tile_p = (ii((n_p, sw), 0) == ii((n_p, sw), 1) % n_p).astype(BF16)
    tile_c = (ii((n_c, gc), 0) == ii((n_c, gc), 1) % n_c).astype(BF16)
    wide = lambda x, tile: jnp.dot(x.astype(BF16), tile, preferred_element_type=F32)
    lane_grp = ii((n_p, gc), 1) // n_c
    p_re = jnp.ones_like(a_re)
    p_im = jnp.zeros_like(a_re)
    for tau in range(S5_CHUNK + 1):
        pre_ref[:, tau, :] = p_re
        pim_ref[:, tau, :] = p_im
        m_re = c_re * p_re[:, None, :] - c_im * p_im[:, None, :]
        m_im = c_re * p_im[:, None, :] + c_im * p_re[:, None, :]
        if tau < S5_CHUNK:
            s = S5_CHUNK - 1 - tau
            pb_re = p_re[:, None, :] * bb_re - p_im[:, None, :] * bb_im
            pb_im = p_re[:, None, :] * bb_im + p_im[:, None, :] * bb_re
            for ri, pb in enumerate((pb_re, pb_im)):
                blk = jnp.where(same((gc, sw), n_c, n_p), wide(pb.reshape(gc, n_p), tile_p), 0.0)
                gm_ref[0, s * gc:(s + 1) * gc, ri * sw:(ri + 1) * sw] = blk.astype(BF16)
            lhs = jnp.concatenate([m_re, -m_im], axis=2)
            k = jnp.einsum("gmk,gnk->gmn", lhs, rhs, precision=lax.Precision.HIGHEST,
                           preferred_element_type=F32)
            kw = jnp.where(same((gc, gc), n_c, n_c), wide(k.reshape(gc, n_c), tile_c), 0.0)
            bd_ref[0, tau] = kw.T.astype(BF16)
        if tau >= 1:
            for ri, m in enumerate((m_re, -m_im)):
                m2 = m.reshape(gc, n_p).astype(BF16).astype(F32)
                mt = jnp.concatenate([m2, jnp.zeros_like(m2)], axis=1).T[:n_p]
                for g in range(gb):
                    hm_ref[0, ri * sw + g * n_p:ri * sw + (g + 1) * n_p, (tau - 1) * gc:tau * gc] = (
                        jnp.where(lane_grp == g, mt, 0.0).astype(BF16))
        p_re, p_im = p_re * ab_re - p_im * ab_im, p_re * ab_im + p_im * ab_re


def _s5_setup(a_re, a_im, log_dt, b_re_t, b_im_t, c_re, c_im):
    g, p = a_re.shape
    c = c_re.shape[1]
    gb = LANES // c
    n_unit = g // gb
    nt = S5_CHUNK + 1
    lw = S5_CHUNK * LANES
    sw = gb * p
    g2 = pl.BlockSpec((gb, p), lambda i: (i, 0))
    g3 = pl.BlockSpec((gb, c, p), lambda i: (i, 0, 0))
    o3 = pl.BlockSpec((gb, nt, p), lambda i: (i, 0, 0))
    s3 = jax.ShapeDtypeStruct((g, nt, p), F32)
    return pl.pallas_call(
        _s5_setup_kernel,
        grid=(n_unit,),
        in_specs=[g2, g2, pl.BlockSpec((gb, 1), lambda i: (i, 0)), g3, g3, g3, g3],
        out_specs=[pl.BlockSpec((1, S5_CHUNK, LANES, LANES), lambda i: (i, 0, 0, 0)),
                   pl.BlockSpec((1, lw, 2 * sw), lambda i: (i, 0, 0)),
                   pl.BlockSpec((1, 2 * sw, lw), lambda i: (i, 0, 0)), o3, o3],
        out_shape=[jax.ShapeDtypeStruct((n_unit, S5_CHUNK, LANES, LANES), BF16),
                   jax.ShapeDtypeStruct((n_unit, lw, 2 * sw), BF16),
                   jax.ShapeDtypeStruct((n_unit, 2 * sw, lw), BF16), s3, s3],
        compiler_params=_params("parallel"),
        name="s5_setup",
    )(a_re, a_im, log_dt, b_re_t, b_im_t, c_re, c_im)


def _s5_kernel(u_ref, h0r_ref, h0i_ref, bd_ref, gm_ref, hm_ref, ar_ref, ai_ref,
               y_ref, her_ref, hei_ref, kt_scr, hs_scr, *, chunk, n_chunk):
    sw = ar_ref.shape[-1]
    rows = u_ref.shape[0] // chunk

    kt_scr[...] = jnp.zeros_like(kt_scr)
    for s in range(chunk):
        for t in range(s, chunk):
            kt_scr[s * LANES:(s + 1) * LANES, t * LANES:(t + 1) * LANES] = bd_ref[0, t - s]

    ucat = jnp.concatenate(
        [u_ref[pl.ds(s, rows, stride=chunk), :] for s in range(chunk)], axis=1).astype(BF16)
    gu = jnp.dot(ucat, gm_ref[0], preferred_element_type=F32)
    a_r = ar_ref[0]
    a_i = ai_ref[0]

    def advance(h_r, h_i, g):
        return a_r * h_r - a_i * h_i + g[:, :sw], a_r * h_i + a_i * h_r + g[:, sw:]

    if n_chunk == 1:
        h_r = h0r_ref[...]
        h_i = h0i_ref[...]
        hs = jnp.concatenate([h_r, h_i], axis=1)
        e_r, e_i = advance(h_r, h_i, gu)
        her_ref[...] = e_r
        hei_ref[...] = e_i
    else:
        hs_scr[...] = gu
        group = SUBLANES if n_chunk % SUBLANES == 0 else n_chunk
        sub = lax.broadcasted_iota(jnp.int32, (group, 2 * sw), 0)
        n_seq = rows // n_chunk

        def tile(it, hs_all):
            out = []
            for b in range(n_seq):
                h_r, h_i = hs_all[b]
                off = pl.multiple_of(b * n_chunk + it * group, group)
                g8 = hs_scr[pl.ds(off, group), :]
                hs8 = jnp.zeros((group, 2 * sw), F32)
                for i in range(group):
                    hs8 = jnp.where(sub == i, jnp.concatenate([h_r, h_i], axis=1), hs8)
                    h_r, h_i = advance(h_r, h_i, g8[i:i + 1, :])
                hs_scr[pl.ds(off, group), :] = hs8
                out.append((h_r, h_i))
            return tuple(out)

        ends = lax.fori_loop(0, n_chunk // group, tile,
                             tuple((h0r_ref[b], h0i_ref[b]) for b in range(n_seq)))
        for b in range(n_seq):
            her_ref[b] = ends[b][0]
            hei_ref[b] = ends[b][1]
        hs = hs_scr[...]
    y = (jnp.dot(ucat, kt_scr[...], preferred_element_type=F32)
         + jnp.dot(hs.astype(BF16), hm_ref[0], preferred_element_type=F32))
    for t in range(chunk):
        y_ref[pl.ds(t, rows, stride=chunk), :] = y[:, t * LANES:(t + 1) * LANES]


def _s5_conv(u, h0r, h0i, bd, gm, hm, ar, ai, b, t, chunk):
    n, d = u.shape
    n_unit = d // LANES
    sw = ar.shape[-1]
    n_chunk = t // chunk
    lw = chunk * LANES
    if n_chunk == 1:
        hspec = pl.BlockSpec((b, sw), lambda q: (0, q))
        hshape = jax.ShapeDtypeStruct((b, n_unit * sw), F32)
    else:
        h0r, h0i = h0r[:, None, :], h0i[:, None, :]
        hspec = pl.BlockSpec((b, 1, sw), lambda q: (0, 0, q))
        hshape = jax.ShapeDtypeStruct((b, 1, n_unit * sw), F32)
    rows = b * n_chunk
    tok = pl.BlockSpec((n, LANES), lambda q: (0, q))
    y, he_r, he_i = pl.pallas_call(
        functools.partial(_s5_kernel, chunk=chunk, n_chunk=n_chunk),
        grid=(n_unit,),
        in_specs=[
            pl.BlockSpec((n, LANES), lambda q: (0, q), pipeline_mode=pl.Buffered(1)), hspec, hspec,
            pl.BlockSpec((1, chunk, LANES, LANES), lambda q: (q, 0, 0, 0)),
            pl.BlockSpec((1, lw, 2 * sw), lambda q: (q, S5_CHUNK // chunk - 1, 0)),
            pl.BlockSpec((1, 2 * sw, lw), lambda q: (q, 0, 0)),
            pl.BlockSpec((1, 1, sw), lambda q: (q, 0, 0)),
            pl.BlockSpec((1, 1, sw), lambda q: (q, 0, 0)),
        ],
        out_specs=[tok, hspec, hspec],
        out_shape=[jax.ShapeDtypeStruct((n, d), F32), hshape, hshape],
        scratch_shapes=[pltpu.VMEM((lw, lw), BF16), pltpu.VMEM((rows, 2 * sw), F32)],
        compiler_params=_params("parallel"),
        name="s5_conv",
    )(u, h0r, h0i, bd, gm, hm, ar, ai)
    return y, he_r.reshape(b, -1), he_i.reshape(b, -1)


def _s5_out_kernel(x_ref, u_ref, yc_ref, d_ref, w_ref, b_ref, o_ref):
    z = jax.nn.gelu(yc_ref[...] + d_ref[...] * u_ref[...])
    o_ref[...] = x_ref[...] + z * jax.nn.sigmoid(_bdot(z, w_ref[...]) + b_ref[...])


def _s5_out(x, u, yc, d, w, b):
    n, dm = x.shape
    tm = _row_tile(n, 256)
    row = lambda i: (0, 0)
    tile = pl.BlockSpec((tm, dm), lambda i: (i, 0))
    vec = pl.BlockSpec((1, dm), row)
    return pl.pallas_call(
        _s5_out_kernel,
        grid=(n // tm,),
        in_specs=[tile, tile, tile, vec, pl.BlockSpec((dm, dm), row), vec],
        out_specs=tile,
        out_shape=jax.ShapeDtypeStruct((n, dm), F32),
        compiler_params=_params("parallel"),
        name="s5_out",
    )(x, u, yc, d, w, b)


def _block_diag_tiles(w, per_tile):
    n_blk, h, _ = w.shape
    n_tile = n_blk // per_tile
    w = w.reshape(n_tile, per_tile, h, h)
    eye = jnp.eye(per_tile, dtype=w.dtype)
    out = jnp.einsum("tphk,pq->tphqk", w, eye)
    return out.reshape(n_tile, per_tile * h, per_tile * h)


def _head_sum_mats(wa):
    onehot = (jnp.arange(wa)[:, None] // HEAD == jnp.arange(LANES)[None, :])
    return onehot.astype(BF16), onehot.T.astype(BF16)


def _wkv_on_lanes(vecs, v, s0):
    b, t, wa = v.shape
    h = wa // HEAD
    t8 = t // SUBLANES
    tt = min(t, 32)
    if b * h <= LANES:
        rep = LANES // (b * h)
        ip = HEAD // rep
        tp = min(t, LANES)
        keys = [_to_lanes(x, rep, False, tp)[None] for x in vecs]
        v2 = _to_lanes(v, rep, True, tp).reshape(1, t, ip, LANES)
        s2 = s0.reshape(b, h, rep, ip, HEAD).transpose(4, 3, 2, 0, 1).reshape(1, HEAD, ip, LANES)
        y2, s2 = _wkv(keys, v2, s2, tt)
        y = _from_lanes(y2.reshape(t * ip, LANES), b, wa, rep, tp)
        s = s2.reshape(HEAD, ip, rep, b, h).transpose(3, 4, 2, 1, 0).reshape(b, h, HEAD, HEAD)
    else:
        nb = b // LANES
        lay = lambda x: x.reshape(nb, LANES, t8, SUBLANES, h, HEAD).transpose(4, 0, 5, 2, 3, 1).reshape(
            h * nb, HEAD, t8, SUBLANES, LANES)
        v2 = v.reshape(nb, LANES, t, h, HEAD).transpose(3, 0, 2, 4, 1).reshape(h * nb, t, HEAD, LANES)
        s2 = s0.reshape(nb, LANES, h, HEAD, HEAD).transpose(2, 0, 4, 3, 1).reshape(h * nb, HEAD, HEAD, LANES)
        y2, s2 = _wkv([lay(x) for x in vecs], v2, s2, tt)
        y = y2.reshape(h, nb, t, HEAD, LANES).transpose(1, 4, 2, 0, 3).reshape(b, t, wa)
        s = s2.reshape(h, nb, HEAD, HEAD, LANES).transpose(1, 4, 0, 3, 2).reshape(b, h, HEAD, HEAD)
    return y, s


def _seq_blocks(b, t):
    if t >= 256:
        return 1, 256
    return min(b, max(1, 256 // t)), t


def _prepare(w):
    depth = w["norm_ffn1"].shape[0]
    wa = w["w0_a"].shape[1]
    wb = w["lam_b"].shape[1]
    cols_a = w["mu_a"].shape[1]
    ca = ((cols_a + 511) // 512) * 512
    row = lambda v: v.reshape(1, -1)
    bf = lambda v: v.astype(BF16)
    pad_a = lambda v: jnp.pad(v, [(0, 0)] * (v.ndim - 1) + [(0, ca - cols_a)])
    layers = []
    for l in range(depth):
        j = l // 2
        q = dict(ple_gate=bf(w["ple_gate"][l]), ple_proj=bf(w["ple_proj"][l]))
        if l % 2 == 0:
            w_in = w["w_in_ab"][j]
            n_q = wb // (2 * LANES)
            w_out = bf(w["w_out_ab"][j])
            q.update(
                w_in_a=bf(pad_a(w_in[:, :cols_a])), w_in_b=bf(w_in[:, cols_a:]), mu=row(pad_a(w["mu_a"][j])),
                w2=bf(jnp.zeros((LANES, wa), F32).at[:LORA_W].set(w["w2_a"][j])),
                a2=bf(jnp.zeros((LANES, wa), F32).at[LORA_W:LORA_W + LORA_A].set(w["a2_a"][j])),
                g2=bf(jnp.zeros((2 * LANES, wa), F32).at[:LORA_G].set(w["g2_a"][j])),
                wa4=bf(_block_diag_tiles(w["wa_b"][j], (wb // HEAD) // n_q)),
                wx4=bf(_block_diag_tiles(w["wx_b"][j], (wb // HEAD) // n_q)),
                w_out_a=w_out[:wa], w_out_b=w_out[wa:])
        else:
            q.update(
                tabs=_s5_setup(w["a_re_c"][j], w["a_im_c"][j], w["log_dt_c"][j][:, None],
                               w["b_re_c"][j].transpose(0, 2, 1), w["b_im_c"][j].transpose(0, 2, 1),
                               w["c_re_c"][j], w["c_im_c"][j]),
                w_glu=bf(w["w_glu_c"][j]))
        layers.append(q)
    return layers


def _trunk(x, p, st_wkv, st_shift, st_h, st_conv, st_cre, st_cim, w, wp):
    b, t, d = x.shape
    n = b * t
    depth = w["norm_ffn1"].shape[0]
    wa = w["w0_a"].shape[1]
    wb = w["lam_b"].shape[1]
    cols_a = w["mu_a"].shape[1]
    ca = wp[0]["mu"].shape[1]
    n_head = wa // HEAD
    n_grp = w["a_re_c"].shape[1]
    chunk = min(S5_CHUNK, t)
    ered, eexp = _head_sum_mats(wa)
    row = lambda v: v.reshape(1, -1)

    x = x.reshape(n, d)
    new = {k: [] for k in ("wkv", "shift", "h", "conv", "cre", "cim")}
    for l in range(depth):
        j = l // 2
        q = wp[l]
        x = _ffn(x, row(w["norm_ffn1"][l]), w["ffn1_wg"], w["ffn1_wu"], w["ffn1_wd"], l)
        if l % 2 == 0:
            za, zb = _inproj(x, row(w["norm_mix"][l]), q["w_in_a"], q["w_in_b"])
            z3 = za.reshape(b, t, ca)
            zb3 = zb.reshape(b, t, 2 * wb)
            shift = jnp.pad(st_shift[j], ((0, 0), (0, ca - cols_a)))[:, None, :]
            bb, tt = _seq_blocks(b, t)
            r, dec, k2, v, nkk, kka, g = _rwkv_prep(
                z3, shift, q["mu"], row(w["w0_a"][j]), q["w2"], row(w["a0_a"][j]), q["a2"], q["g2"],
                row(w["kk_a"][j]), row(w["ka_a"][j]), ered, eexp, wa, ca, bb, tt)
            y, s_new = _wkv_on_lanes([nkk, dec, kka, k2, r], v, st_wkv[j])
            hist0 = jnp.pad(st_conv[j], ((0, 0), (SUBLANES - (CONV_W - 1), 0), (0, 0)))
            yb, h_new = _rglru(
                zb3, st_h[j][:, None, :], hist0, w["conv_w_b"][j], row(w["conv_b_b"][j]), q["wa4"],
                row(w["ba_b"][j]), q["wx4"], row(w["bx_b"][j]), row(w["lam_b"][j]), bb, tt)
            f2 = lambda a: a.reshape(n, -1)
            x = _mix_out(x, f2(y), f2(r), f2(k2), f2(v), f2(g), f2(yb), row(w["rk_a"][j]),
                         row(w["lnx_g"][j]), row(w["lnx_b"][j]), ered, eexp, q["w_out_a"], q["w_out_b"])
            new["wkv"].append(s_new)
            new["shift"].append(z3[:, t - 1, :cols_a])
            new["h"].append(h_new[:, 0, :])
            conv_all = jnp.concatenate([st_conv[j], zb3[:, max(t - (CONV_W - 1), 0):, :wb]], axis=1)
            new["conv"].append(conv_all[:, conv_all.shape[1] - (CONV_W - 1):])
        else:
            u = _rmsnorm(x, row(w["norm_mix"][l]))
            bd, gm, hm, p_re, p_im = q["tabs"]
            ar = p_re[:, chunk].reshape(bd.shape[0], 1, -1)
            ai = p_im[:, chunk].reshape(bd.shape[0], 1, -1)
            yc, he_r, he_i = _s5_conv(u, st_cre[j].reshape(b, -1), st_cim[j].reshape(b, -1),
                                      bd, gm, hm, ar, ai, b, t, chunk)
            x = _s5_out(x, u, yc, row(w["d_c"][j]), q["w_glu"], row(w["b_glu_c"][j]))
            new["cre"].append(he_r.reshape(b, n_grp, P_C))
            new["cim"].append(he_i.reshape(b, n_grp, P_C))
        x = _ffn(x, row(w["norm_ffn2"][l]), w["ffn2_wg"], w["ffn2_wu"], w["ffn2_wd"], l)
        x = _ple(x, row(w["norm_ple"][l]), q["ple_gate"], p[l].reshape(n, -1), q["ple_proj"],
                 row(w["final_norm"]) if l == depth - 1 else None)
    y = x.reshape(b, t, d)
    stk = lambda name, ref: jnp.stack(new[name]).astype(ref.dtype)
    return y, (stk("wkv", st_wkv), stk("shift", st_shift), stk("h", st_h),
               stk("conv", st_conv), stk("cre", st_cre), stk("cim", st_cim))


def kernel(x_prompt, x_sample, state_a_wkv, state_a_shift, state_b_h, state_b_conv, state_c_re, state_c_im, p_prompt, p_sample, norm_ffn1, ffn1_wg, ffn1_wu, ffn1_wd, norm_mix, norm_ffn2, ffn2_wg, ffn2_wu, ffn2_wd, norm_ple, ple_gate, ple_proj, w_in_ab, mu_a, w0_a, w2_a, a0_a, a2_a, g2_a, kk_a, ka_a, rk_a, lnx_g, lnx_b, conv_w_b, conv_b_b, wa_b, ba_b, wx_b, bx_b, lam_b, w_out_ab, a_re_c, a_im_c, log_dt_c, b_re_c, b_im_c, c_re_c, c_im_c, d_c, w_glu_c, b_glu_c, final_norm):
    w = dict(norm_ffn1=norm_ffn1, ffn1_wg=ffn1_wg, ffn1_wu=ffn1_wu, ffn1_wd=ffn1_wd,
             norm_mix=norm_mix, norm_ffn2=norm_ffn2, ffn2_wg=ffn2_wg, ffn2_wu=ffn2_wu,
             ffn2_wd=ffn2_wd, norm_ple=norm_ple, ple_gate=ple_gate, ple_proj=ple_proj,
             w_in_ab=w_in_ab, mu_a=mu_a, w0_a=w0_a, w2_a=w2_a, a0_a=a0_a, a2_a=a2_a, g2_a=g2_a,
             kk_a=kk_a, ka_a=ka_a, rk_a=rk_a.reshape(rk_a.shape[0], -1), lnx_g=lnx_g, lnx_b=lnx_b,
             conv_w_b=conv_w_b, conv_b_b=conv_b_b, wa_b=wa_b, ba_b=ba_b, wx_b=wx_b, bx_b=bx_b,
             lam_b=lam_b, w_out_ab=w_out_ab,
             a_re_c=a_re_c, a_im_c=a_im_c, log_dt_c=log_dt_c, b_re_c=b_re_c, b_im_c=b_im_c,
             c_re_c=c_re_c, c_im_c=c_im_c, d_c=d_c, w_glu_c=w_glu_c, b_glu_c=b_glu_c,
             final_norm=final_norm)
    wp = _prepare(w)
    bp = x_prompt.shape[0]
    zeros = lambda s: jnp.zeros((s.shape[0], bp) + s.shape[2:], s.dtype)
    y_prompt, prompt_state = _trunk(
        x_prompt, p_prompt, zeros(state_a_wkv), zeros(state_a_shift), zeros(state_b_h),
        zeros(state_b_conv), zeros(state_c_re), zeros(state_c_im), w, wp)
    y_sample, sample_state = _trunk(
        x_sample, p_sample, state_a_wkv, state_a_shift, state_b_h, state_b_conv,
        state_c_re, state_c_im, w, wp)
    return (y_prompt, y_sample) + tuple(prompt_state) + tuple(sample_state)
```

```python
import functools

import jax
import jax.numpy as jnp
from jax import lax
from jax.experimental import pallas as pl
from jax.experimental.pallas import tpu as pltpu

F32 = jnp.float32
BF16 = jnp.bfloat16

HEAD = 64
LANES = 128
SUBLANES = 8
LORA_W = 64
LORA_A = 64
LORA_G = 160
CONV_W = 4
LRU_C = 8.0
GRP_C = 16
P_C = 64
RMS_EPS = 1e-6
GN_EPS = 64e-5
S5_CHUNK = 16
N_ACC = 4
VMEM_LIMIT = 56 * 1024 * 1024


def _params(*sem):
    return pltpu.CompilerParams(dimension_semantics=sem, vmem_limit_bytes=VMEM_LIMIT)


def _rms(x, g):
    return x * lax.rsqrt(jnp.mean(x * x, axis=-1, keepdims=True) + RMS_EPS) * g


def _softplus(x):
    return jnp.maximum(x, 0.0) + jnp.log1p(jnp.exp(-jnp.abs(x)))


def _bdot(a, b):
    return jnp.dot(a.astype(BF16), b, preferred_element_type=F32)


def _split_dot(x, e):
    hi = x.astype(BF16)
    lo = (x - hi.astype(F32)).astype(BF16)
    return (jnp.dot(hi, e, preferred_element_type=F32)
            + jnp.dot(lo, e, preferred_element_type=F32))


def _head_sum(x, e_red, e_exp):
    return _split_dot(_split_dot(x, e_red), e_exp)


def _row_tile(n, pref):
    t = min(n, pref)
    while n % t:
        t //= 2
    return t


def _ffn_kernel(x_ref, g_ref, wg_ref, wu_ref, wd_ref, o_ref, h_ref):
    @pl.when(pl.program_id(1) == 0)
    def _():
        x = x_ref[...]
        h_ref[...] = _rms(x, g_ref[...]).astype(BF16)
        o_ref[...] = x

    h = h_ref[...]
    a = jnp.dot(h, wg_ref[...].astype(BF16), preferred_element_type=F32)
    b = jnp.dot(h, wu_ref[...].astype(BF16), preferred_element_type=F32)
    hid = (0.5 * a * jax.nn.sigmoid(a)) * b
    o_ref[...] += jnp.dot(hid.astype(BF16), wd_ref[...].astype(BF16), preferred_element_type=F32)


def _ffn(x, g, wg, wu, wd, l):
    n, d = x.shape
    f = wg.shape[2]
    tm = _row_tile(n, 1024)
    tf = _row_tile(f, 256)
    return pl.pallas_call(
        _ffn_kernel,
        grid=(n // tm, f // tf),
        in_specs=[
            pl.BlockSpec((tm, d), lambda i, j: (i, 0), pipeline_mode=pl.Buffered(1)),
            pl.BlockSpec((1, d), lambda i, j: (0, 0)),
            pl.BlockSpec((None, d, tf), lambda i, j: (l, 0, j)),
            pl.BlockSpec((None, d, tf), lambda i, j: (l, 0, j)),
            pl.BlockSpec((None, tf, d), lambda i, j: (l, j, 0)),
        ],
        out_specs=pl.BlockSpec((tm, d), lambda i, j: (i, 0)),
        out_shape=jax.ShapeDtypeStruct((n, d), F32),
        scratch_shapes=[pltpu.VMEM((tm, d), BF16)],
        compiler_params=_params("parallel", "arbitrary"),
        name="ffn",
    )(x, g, wg, wu, wd)


def _ple_kernel(x_ref, g_ref, wgate_ref, p_ref, wproj_ref, *rest):
    o_ref = rest[-1]
    x = x_ref[...]
    gate = jax.nn.sigmoid(_bdot(_rms(x, g_ref[...]), wgate_ref[...]))
    y = x + gate * _bdot(p_ref[...], wproj_ref[...])
    o_ref[...] = _rms(y, rest[0][...]) if len(rest) == 2 else y


def _ple(x, g, wgate, p, wproj, final_g=None):
    n, d = x.shape
    dp = p.shape[1]
    tm = _row_tile(n, 512)
    vec = pl.BlockSpec((1, d), lambda i: (0, 0))
    extra = [] if final_g is None else [final_g]
    return pl.pallas_call(
        _ple_kernel,
        grid=(n // tm,),
        in_specs=[
            pl.BlockSpec((tm, d), lambda i: (i, 0)),
            vec,
            pl.BlockSpec((d, d), lambda i: (0, 0)),
            pl.BlockSpec((tm, dp), lambda i: (i, 0)),
            pl.BlockSpec((dp, d), lambda i: (0, 0)),
        ] + [vec] * len(extra),
        out_specs=pl.BlockSpec((tm, d), lambda i: (i, 0)),
        out_shape=jax.ShapeDtypeStruct((n, d), F32),
        compiler_params=_params("parallel"),
        name="ple",
    )(x, g, wgate, p, wproj, *extra)


def _rmsnorm_kernel(x_ref, g_ref, o_ref):
    o_ref[...] = _rms(x_ref[...], g_ref[...])


def _rmsnorm(x, g):
    n, d = x.shape
    tm = _row_tile(n, 512)
    return pl.pallas_call(
        _rmsnorm_kernel,
        grid=(n // tm,),
        in_specs=[pl.BlockSpec((tm, d), lambda i: (i, 0)),
                  pl.BlockSpec((1, d), lambda i: (0, 0))],
        out_specs=pl.BlockSpec((tm, d), lambda i: (i, 0)),
        out_shape=jax.ShapeDtypeStruct((n, d), F32),
        compiler_params=_params("parallel"),
        name="rmsnorm",
    )(x, g)


def _inproj_kernel(x_ref, g_ref, wa_ref, wb_ref, za_ref, zb_ref):
    h = _rms(x_ref[...], g_ref[...]).astype(BF16)
    za_ref[...] = jnp.dot(h, wa_ref[...], preferred_element_type=F32)
    zb_ref[...] = jnp.dot(h, wb_ref[...], preferred_element_type=F32)


def _inproj(x, g, w_a, w_b):
    n, d = x.shape
    ca, cb = w_a.shape[1], w_b.shape[1]
    tm = _row_tile(n, 256)
    fixed = lambda shape: pl.BlockSpec(shape, lambda i: (0, 0), pipeline_mode=pl.Buffered(1))
    return pl.pallas_call(
        _inproj_kernel,
        grid=(n // tm,),
        in_specs=[pl.BlockSpec((tm, d), lambda i: (i, 0)), pl.BlockSpec((1, d), lambda i: (0, 0)),
                  fixed((d, ca)), fixed((d, cb))],
        out_specs=[pl.BlockSpec((tm, ca), lambda i: (i, 0)), pl.BlockSpec((tm, cb), lambda i: (i, 0))],
        out_shape=[jax.ShapeDtypeStruct((n, ca), F32), jax.ShapeDtypeStruct((n, cb), F32)],
        compiler_params=_params("parallel"),
        name="inproj",
    )(x, g, w_a, w_b)


def _rwkv_prep_kernel(z_ref, shift_ref, mu_ref, w0_ref, w2_ref, a0_ref, a2_ref, g2_ref,
                      kk_ref, ka_ref, ered_ref, eexp_ref,
                      r_ref, w_ref, k_ref, v_ref, nkk_ref, kka_ref, g_ref, carry_ref, *, wa):
    tb = pl.program_id(1)
    z = z_ref[...]
    bb, tt, ca = z.shape

    @pl.when(tb == 0)
    def _():
        carry_ref[...] = shift_ref[...]

    t_idx = lax.broadcasted_iota(jnp.int32, z.shape, 1)
    zprev = jnp.where(t_idx == 0, carry_ref[...], pltpu.roll(z, 1, axis=1))
    carry_ref[...] = z[:, tt - 1:tt, :]
    zs = (z + (zprev - z) * mu_ref[...]).reshape(bb * tt, ca)

    r = zs[:, 0:wa]
    k = zs[:, wa:2 * wa]
    v = zs[:, 2 * wa:3 * wa]
    xwa = zs[:, 3 * wa:3 * wa + LANES]
    xg = zs[:, 3 * wa + LANES:3 * wa + 3 * LANES]
    w_log = -_softplus(-(w0_ref[...] + _bdot(jnp.tanh(xwa), w2_ref[...]))) - 0.5
    decay = jnp.exp(-jnp.exp(w_log))
    a = jax.nn.sigmoid(a0_ref[...] + _bdot(xwa, a2_ref[...]))
    g = _bdot(jax.nn.sigmoid(xg), g2_ref[...])
    kk = k * kk_ref[...]
    kn = kk / jnp.maximum(jnp.sqrt(_head_sum(kk * kk, ered_ref[...], eexp_ref[...])), 1e-12)
    k2 = k * (1.0 + (a - 1.0) * ka_ref[...])
    shp = (bb, tt, wa)
    r_ref[...] = r.reshape(shp)
    w_ref[...] = decay.reshape(shp)
    k_ref[...] = k2.reshape(shp)
    v_ref[...] = v.reshape(shp)
    nkk_ref[...] = (-kn).reshape(shp)
    kka_ref[...] = (kn * a).reshape(shp)
    g_ref[...] = g.reshape(shp)


def _rwkv_prep(z3, shift, mu, w0, w2p, a0, a2p, g2p, kk, ka, ered, eexp, wa, ca, bb, tt):
    b, t, _ = z3.shape
    row = lambda i, j: (0, 0)
    vec = lambda c: pl.BlockSpec((1, c), row)
    out_spec = pl.BlockSpec((bb, tt, wa), lambda i, j: (i, j, 0))
    out_shape = jax.ShapeDtypeStruct((b, t, wa), F32)
    return pl.pallas_call(
        functools.partial(_rwkv_prep_kernel, wa=wa),
        grid=(b // bb, t // tt),
        in_specs=[
            pl.BlockSpec((bb, tt, ca), lambda i, j: (i, j, 0)),
            pl.BlockSpec((bb, 1, ca), lambda i, j: (i, 0, 0)),
            vec(ca), vec(wa),
            pl.BlockSpec(w2p.shape, row), vec(wa),
            pl.BlockSpec(a2p.shape, row), pl.BlockSpec(g2p.shape, row),
            vec(wa), vec(wa),
            pl.BlockSpec(ered.shape, row), pl.BlockSpec(eexp.shape, row),
        ],
        out_specs=[out_spec] * 7,
        out_shape=[out_shape] * 7,
        scratch_shapes=[pltpu.VMEM((bb, 1, ca), F32)],
        compiler_params=_params("parallel", "arbitrary"),
        name="rwkv_prep",
    )(z3, shift, mu, w0, w2p, a0, a2p, g2p, kk, ka, ered, eexp)


def _to_lanes_kernel(x_ref, o_ref, xt_scr, *, rep, split):
    bsz, tt, wa = x_ref.shape
    n_head = wa // HEAD
    n_rows = HEAD // rep if split else HEAD
    for b in range(bsz):
        xt_scr[b] = x_ref[b].T
    for r in range(n_rows):
        offs = [r + q * n_rows for q in range(rep)] if split else [r] * rep
        m = jnp.concatenate(
            [xt_scr[b, pl.ds(off, n_head, stride=HEAD), :] for off in offs for b in range(bsz)], axis=0).T
        if split:
            o_ref[pl.ds(r, tt, stride=n_rows), :] = m
        else:
            o_ref[r] = m.reshape(tt // SUBLANES, SUBLANES, LANES)


def _to_lanes(x, rep, split, tt):
    b, t, wa = x.shape
    n_rows = HEAD // rep if split else HEAD
    if split:
        out_spec = pl.BlockSpec((tt * n_rows, LANES), lambda i: (i, 0))
        out_shape = jax.ShapeDtypeStruct((t * n_rows, LANES), F32)
    else:
        out_spec = pl.BlockSpec((HEAD, tt // SUBLANES, SUBLANES, LANES), lambda i: (0, i, 0, 0))
        out_shape = jax.ShapeDtypeStruct((HEAD, t // SUBLANES, SUBLANES, LANES), F32)
    return pl.pallas_call(
        functools.partial(_to_lanes_kernel, rep=rep, split=split),
        grid=(t // tt,),
        in_specs=[pl.BlockSpec((b, tt, wa), lambda i: (0, i, 0))],
        out_specs=out_spec,
        out_shape=out_shape,
        scratch_shapes=[pltpu.VMEM((b, wa, tt), F32)],
        compiler_params=_params("parallel"),
        name="to_lanes",
    )(x)


def _from_lanes_kernel(y_ref, o_ref, yt_scr, *, rep):
    bsz, tt, wa = o_ref.shape
    n_head = wa // HEAD
    n_rows = HEAD // rep
    for r in range(n_rows):
        m = y_ref[pl.ds(r, tt, stride=n_rows), :].T
        for q in range(rep):
            for b in range(bsz):
                k0 = (q * bsz + b) * n_head
                yt_scr[b, pl.ds(q * n_rows + r, n_head, stride=HEAD), :] = m[k0:k0 + n_head, :]
    for b in range(bsz):
        o_ref[b] = yt_scr[b].T


def _from_lanes(y2, b, wa, rep, tt):
    n_rows = HEAD // rep
    t = y2.shape[0] // n_rows
    return pl.pallas_call(
        functools.partial(_from_lanes_kernel, rep=rep),
        grid=(t // tt,),
        in_specs=[pl.BlockSpec((tt * n_rows, LANES), lambda i: (i, 0))],
        out_specs=pl.BlockSpec((b, tt, wa), lambda i: (0, i, 0)),
        out_shape=jax.ShapeDtypeStruct((b, t, wa), F32),
        scratch_shapes=[pltpu.VMEM((b, wa, tt), F32)],
        compiler_params=_params("parallel"),
        name="from_lanes",
    )(y2)


def _wkv_kernel(nkk_ref, w_ref, kka_ref, k_ref, r_ref, v_ref, s0_ref, y_ref, s_ref):
    n_key, n_t8 = nkk_ref.shape[1], nkk_ref.shape[2]
    slab = s_ref.shape[2:]

    @pl.when(pl.program_id(1) == 0)
    def _():
        s_ref[...] = s0_ref[...]

    def tree_sum(parts):
        while len(parts) > 1:
            parts = [parts[i] + parts[i + 1] for i in range(0, len(parts), 2)]
        return parts[0]

    def tile_step(t8, carry):
        for i in range(SUBLANES):
            t = t8 * SUBLANES + i
            key_row = lambda ref, j: ref[0, j, t8, i:i + 1, :]
            acc = [jnp.zeros(slab, F32) for _ in range(N_ACC)]
            for j in range(n_key):
                acc[j % N_ACC] = acc[j % N_ACC] + s_ref[0, j] * key_row(nkk_ref, j)
            sa = tree_sum(acc)
            v = v_ref[0, t]
            acc = [jnp.zeros(slab, F32) for _ in range(N_ACC)]
            for j in range(n_key):
                s = s_ref[0, j] * key_row(w_ref, j) + sa * key_row(kka_ref, j) + v * key_row(k_ref, j)
                s_ref[0, j] = s
                acc[j % N_ACC] = acc[j % N_ACC] + s * key_row(r_ref, j)
            y_ref[0, t] = tree_sum(acc)
        return carry

    lax.fori_loop(0, n_t8, tile_step, 0)


def _wkv(keys, v, s0, tt):
    u, t, rows, _ = v.shape
    n_key = s0.shape[1]
    st = pl.BlockSpec((1, n_key, rows, LANES), lambda i, j: (i, 0, 0, 0))
    seq = pl.BlockSpec((1, tt, rows, LANES), lambda i, j: (i, j, 0, 0))
    key = pl.BlockSpec((1, n_key, tt // SUBLANES, SUBLANES, LANES), lambda i, j: (i, 0, j, 0, 0))
    return pl.pallas_call(
        _wkv_kernel,
        grid=(u, t // tt),
        in_specs=[key] * 5 + [seq, st],
        out_specs=[seq, st],
        out_shape=[jax.ShapeDtypeStruct(v.shape, F32), jax.ShapeDtypeStruct(s0.shape, F32)],
        compiler_params=_params("parallel", "arbitrary"),
        name="wkv",
    )(*keys, v, s0)


def _mix_out_kernel(x_ref, y_ref, r_ref, k_ref, v_ref, g_ref, yb_ref, rk_ref, lng_ref, lnb_ref,
                    ered_ref, eexp_ref, woa_ref, wob_ref, o_ref):
    ered = ered_ref[...]
    eexp = eexp_ref[...]
    y = y_ref[...]
    v = v_ref[...]
    mean = _head_sum(y, ered, eexp) * (1.0 / HEAD)
    d = y - mean
    var = _head_sum(d * d, ered, eexp) * (1.0 / HEAD)
    yn = d * lax.rsqrt(var + GN_EPS) * lng_ref[...] + lnb_ref[...]
    bonus = _head_sum(r_ref[...] * k_ref[...] * rk_ref[...], ered, eexp) * v
    ya = (yn + bonus) * g_ref[...]
    o_ref[...] = x_ref[...] + _bdot(ya, woa_ref[...]) + _bdot(yb_ref[...], wob_ref[...])


def _mix_out(x, y, r, k, v, g, yb, rk, lng, lnb, ered, eexp, woa, wob):
    n, d = x.shape
    wa = y.shape[1]
    wb = yb.shape[1]
    tm = _row_tile(n, 256)
    row = lambda i: (0, 0)
    ta = pl.BlockSpec((tm, wa), lambda i: (i, 0))
    va = pl.BlockSpec((1, wa), row)
    return pl.pallas_call(
        _mix_out_kernel,
        grid=(n // tm,),
        in_specs=[pl.BlockSpec((tm, d), lambda i: (i, 0)), ta, ta, ta, ta, ta,
                  pl.BlockSpec((tm, wb), lambda i: (i, 0)), va, va, va,
                  pl.BlockSpec(ered.shape, row), pl.BlockSpec(eexp.shape, row),
                  pl.BlockSpec(woa.shape, row), pl.BlockSpec(wob.shape, row)],
        out_specs=pl.BlockSpec((tm, d), lambda i: (i, 0)),
        out_shape=jax.ShapeDtypeStruct((n, d), F32),
        compiler_params=_params("parallel"),
        name="mix_out",
    )(x, y, r, k, v, g, yb, rk, lng, lnb, ered, eexp, woa, wob)


def _rglru_kernel(xb_ref, gb_ref, h0_ref, hist0_ref, cw_ref, cb_ref, wa_ref, ba_ref, wx_ref, bx_ref,
                  lam_ref, y_ref, hl_ref, hist_ref, h_ref, a_scr, b_scr):
    tb = pl.program_id(1)
    x = xb_ref[...]
    bb, tt, wb = x.shape
    n_tile = tt // SUBLANES

    @pl.when(tb == 0)
    def _():
        hist_ref[...] = hist0_ref[...]
        h_ref[...] = h0_ref[...]

    hist = hist_ref[...]
    hist_ref[...] = x[:, tt - SUBLANES:tt, :]
    cw = cw_ref[...]
    t8 = lax.broadcasted_iota(jnp.int32, (bb, SUBLANES, wb), 1)
    xc = cb_ref[...] + x * cw[CONV_W - 1:CONV_W, :]
    for dly in range(1, CONV_W):
        rolled = pltpu.roll(x, dly, axis=1)
        head = jnp.where(t8 < dly, pltpu.roll(hist, dly, axis=1), rolled[:, :SUBLANES, :])
        if n_tile > 1:
            shifted = jnp.concatenate([head, rolled[:, SUBLANES:, :]], axis=1)
        else:
            shifted = head
        xc = xc + shifted * cw[CONV_W - 1 - dly:CONV_W - dly, :]

    xc2 = xc.reshape(bb * tt, wb)
    xcb = xc2.astype(BF16)
    n_q = wa_ref.shape[0]
    wq = wb // n_q
    gr = jnp.concatenate(
        [jnp.dot(xcb[:, q * wq:(q + 1) * wq], wa_ref[q], preferred_element_type=F32) for q in range(n_q)],
        axis=1)
    gi = jnp.concatenate(
        [jnp.dot(xcb[:, q * wq:(q + 1) * wq], wx_ref[q], preferred_element_type=F32) for q in range(n_q)],
        axis=1)
    gate_r = jax.nn.sigmoid(gr + ba_ref[...])
    gate_i = jax.nn.sigmoid(gi + bx_ref[...])
    log_a = (-LRU_C) * gate_r * _softplus(-lam_ref[...])
    a = jnp.exp(log_a)
    b = jnp.sqrt(1.0 - jnp.exp(2.0 * log_a)) * (gate_i * xc2)
    a_scr[...] = a.reshape(bb, tt, wb)
    b_scr[...] = b.reshape(bb, tt, wb)

    def tile_scan(i, h):
        off = pl.multiple_of(i * SUBLANES, SUBLANES)
        at = a_scr[:, pl.ds(off, SUBLANES), :]
        bt = b_scr[:, pl.ds(off, SUBLANES), :]
        for dly in (1, 2, 4):
            keep = t8 >= dly
            bt = jnp.where(keep, at * pltpu.roll(bt, dly, axis=1) + bt, bt)
            at = jnp.where(keep, at * pltpu.roll(at, dly, axis=1), at)
        ht = bt + at * h
        b_scr[:, pl.ds(off, SUBLANES), :] = ht
        return ht[:, SUBLANES - 1:SUBLANES, :]

    h_last = lax.fori_loop(0, n_tile, tile_scan, h_ref[...])
    h_ref[...] = h_last
    hl_ref[...] = h_last
    y_ref[...] = b_scr[...] * jax.nn.gelu(gb_ref[...])


def _rglru(zb3, h0, hist0, cw, cb, wa4, ba, wx4, bx, lam, bb, tt):
    b, t, _ = zb3.shape
    wb = h0.shape[-1]
    row = lambda i, j: (0, 0)
    vec = pl.BlockSpec((1, wb), row)
    st = pl.BlockSpec((bb, 1, wb), lambda i, j: (i, 0, 0))
    return pl.pallas_call(
        _rglru_kernel,
        grid=(b // bb, t // tt),
        in_specs=[
            pl.BlockSpec((bb, tt, wb), lambda i, j: (i, j, 0)),
            pl.BlockSpec((bb, tt, wb), lambda i, j: (i, j, 1)),
            st,
            pl.BlockSpec((bb, SUBLANES, wb), lambda i, j: (i, 0, 0)),
            pl.BlockSpec((CONV_W, wb), row), vec,
            pl.BlockSpec(wa4.shape, lambda i, j: (0, 0, 0)), vec,
            pl.BlockSpec(wx4.shape, lambda i, j: (0, 0, 0)), vec, vec,
        ],
        out_specs=[pl.BlockSpec((bb, tt, wb), lambda i, j: (i, j, 0)), st],
        out_shape=[jax.ShapeDtypeStruct((b, t, wb), F32), jax.ShapeDtypeStruct((b, 1, wb), F32)],
        scratch_shapes=[pltpu.VMEM((bb, SUBLANES, wb), F32), pltpu.VMEM((bb, 1, wb), F32),
                        pltpu.VMEM((bb, tt, wb), F32), pltpu.VMEM((bb, tt, wb), F32)],
        compiler_params=_params("parallel", "arbitrary"),
        name="rglru",
    )(zb3, zb3, h0, hist0, cw, cb, wa4, ba, wx4, bx, lam)


def _s5_setup_kernel(are_ref, aim_ref, ldt_ref, bre_ref, bim_ref, cre_ref, cim_ref,
                     bd_ref, gm_ref, hm_ref, pre_ref, pim_ref):
    a_re = are_ref[...]
    a_im = aim_ref[...]
    dt = jnp.exp(ldt_ref[...])
    mag = jnp.exp(dt * a_re)
    ab_re = mag * jnp.cos(dt * a_im)
    ab_im = mag * jnp.sin(dt * a_im)
    den = a_re * a_re + a_im * a_im
    f_re = ((ab_re - 1.0) * a_re + ab_im * a_im) / den
    f_im = (ab_im * a_re - (ab_re - 1.0) * a_im) / den
    b_re = bre_ref[...]
    b_im = bim_ref[...]
    bb_re = f_re[:, None, :] * b_re - f_im[:, None, :] * b_im
    bb_im = f_re[:, None, :] * b_im + f_im[:, None, :] * b_re
    c_re = cre_ref[...]
    c_im = cim_ref[...]
    rhs = jnp.concatenate([bb_re, bb_im], axis=2)
    gb, n_c, n_p = c_re.shape
    gc = gb * n_c
    sw = gb * n_p
    ii = lambda shape, dim: lax.broadcasted_iota(jnp.int32, shape, dim)
    same = lambda shape, rdiv, ldiv: ii(shape, 0) // rdiv == ii(shape, 1) // ldiv
    tile_p = (ii((n_p, sw), 0) == ii((n_p, sw), 1) % n_p).astype(BF16)
    tile_c = (ii((n_c, gc), 0) == ii((n_c, gc), 1) % n_c).astype(BF16)
    wide = lambda x, tile: jnp.dot(x.astype(BF16), tile, preferred_element_type=F32)
    lane_grp = ii((n_p, gc), 1) // n_c
    p_re = jnp.ones_like(a_re)
    p_im = jnp.zeros_like(a_re)
    for tau in range(S5_CHUNK + 1):
        pre_ref[:, tau, :] = p_re
        pim_ref[:, tau, :] = p_im
        m_re = c_re * p_re[:, None, :] - c_im * p_im[:, None, :]
        m_im = c_re * p_im[:, None, :] + c_im * p_re[:, None, :]
        if tau < S5_CHUNK:
            s = S5_CHUNK - 1 - tau
            pb_re = p_re[:, None, :] * bb_re - p_im[:, None, :] * bb_im
            pb_im = p_re[:, None, :] * bb_im + p_im[:, None, :] * bb_re
            for ri, pb in enumerate((pb_re, pb_im)):
                blk = jnp.where(same((gc, sw), n_c, n_p), wide(pb.reshape(gc, n_p), tile_p), 0.0)
                gm_ref[0, s * gc:(s + 1) * gc, ri * sw:(ri + 1) * sw] = blk.astype(BF16)
            lhs = jnp.concatenate([m_re, -m_im], axis=2)
            k = jnp.einsum("gmk,gnk->gmn", lhs, rhs, precision=lax.Precision.HIGHEST,
                           preferred_element_type=F32)
            kw = jnp.where(same((gc, gc), n_c, n_c), wide(k.reshape(gc, n_c), tile_c), 0.0)
            bd_ref[0, tau] = kw.T.astype(BF16)
        if tau >= 1:
            for ri, m in enumerate((m_re, -m_im)):
                m2 = m.reshape(gc, n_p).astype(BF16).astype(F32)
                mt = jnp.concatenate([m2, jnp.zeros_like(m2)], axis=1).T[:n_p]
                for g in range(gb):
                    hm_ref[0, ri * sw + g * n_p:ri * sw + (g + 1) * n_p, (tau - 1) * gc:tau * gc] = (
                        jnp.where(lane_grp == g, mt, 0.0).astype(BF16))
        p_re, p_im = p_re * ab_re - p_im * ab_im, p_re * ab_im + p_im * ab_re


def _s5_setup(a_re, a_im, log_dt, b_re_t, b_im_t, c_re, c_im):
    g, p = a_re.shape
    c = c_re.shape[1]
    gb = LANES // c
    n_unit = g // gb
    nt = S5_CHUNK + 1
    lw = S5_CHUNK * LANES
    sw = gb * p
    g2 = pl.BlockSpec((gb, p), lambda i: (i, 0))
    g3 = pl.BlockSpec((gb, c, p), lambda i: (i, 0, 0))
    o3 = pl.BlockSpec((gb, nt, p), lambda i: (i, 0, 0))
    s3 = jax.ShapeDtypeStruct((g, nt, p), F32)
    return pl.pallas_call(
        _s5_setup_kernel,
        grid=(n_unit,),
        in_specs=[g2, g2, pl.BlockSpec((gb, 1), lambda i: (i, 0)), g3, g3, g3, g3],
        out_specs=[pl.BlockSpec((1, S5_CHUNK, LANES, LANES), lambda i: (i, 0, 0, 0)),
                   pl.BlockSpec((1, lw, 2 * sw), lambda i: (i, 0, 0)),
                   pl.BlockSpec((1, 2 * sw, lw), lambda i: (i, 0, 0)), o3, o3],
        out_shape=[jax.ShapeDtypeStruct((n_unit, S5_CHUNK, LANES, LANES), BF16),
                   jax.ShapeDtypeStruct((n_unit, lw, 2 * sw), BF16),
                   jax.ShapeDtypeStruct((n_unit, 2 * sw, lw), BF16), s3, s3],
        compiler_params=_params("parallel"),
        name="s5_setup",
    )(a_re, a_im, log_dt, b_re_t, b_im_t, c_re, c_im)


def _s5_kernel(u_ref, h0r_ref, h0i_ref, bd_ref, gm_ref, hm_ref, ar_ref, ai_ref,
               y_ref, her_ref, hei_ref, kt_scr, hs_scr, *, chunk, n_chunk):
    sw = ar_ref.shape[-1]
    rows = u_ref.shape[0] // chunk

    kt_scr[...] = jnp.zeros_like(kt_scr)
    for s in range(chunk):
        for t in range(s, chunk):
            kt_scr[s * LANES:(s + 1) * LANES, t * LANES:(t + 1) * LANES] = bd_ref[0, t - s]

    ucat = jnp.concatenate(
        [u_ref[pl.ds(s, rows, stride=chunk), :] for s in range(chunk)], axis=1).astype(BF16)
    gu = jnp.dot(ucat, gm_ref[0], preferred_element_type=F32)
    a_r = ar_ref[0]
    a_i = ai_ref[0]

    def advance(h_r, h_i, g):
        return a_r * h_r - a_i * h_i + g[:, :sw], a_r * h_i + a_i * h_r + g[:, sw:]

    if n_chunk == 1:
        h_r = h0r_ref[...]
        h_i = h0i_ref[...]
        hs = jnp.concatenate([h_r, h_i], axis=1)
        e_r, e_i = advance(h_r, h_i, gu)
        her_ref[...] = e_r
        hei_ref[...] = e_i
    else:
        hs_scr[...] = gu
        group = SUBLANES if n_chunk % SUBLANES == 0 else n_chunk
        sub = lax.broadcasted_iota(jnp.int32, (group, 2 * sw), 0)
        n_seq = rows // n_chunk

        def tile(it, hs_all):
            out = []
            for b in range(n_seq):
                h_r, h_i = hs_all[b]
                off = pl.multiple_of(b * n_chunk + it * group, group)
                g8 = hs_scr[pl.ds(off, group), :]
                hs8 = jnp.zeros((group, 2 * sw), F32)
                for i in range(group):
                    hs8 = jnp.where(sub == i, jnp.concatenate([h_r, h_i], axis=1), hs8)
                    h_r, h_i = advance(h_r, h_i, g8[i:i + 1, :])
                hs_scr[pl.ds(off, group), :] = hs8
                out.append((h_r, h_i))
            return tuple(out)

        ends = lax.fori_loop(0, n_chunk // group, tile,
                             tuple((h0r_ref[b], h0i_ref[b]) for b in range(n_seq)))
        for b in range(n_seq):
            her_ref[b] = ends[b][0]
            hei_ref[b] = ends[b][1]
        hs = hs_scr[...]
    y = (jnp.dot(ucat, kt_scr[...], preferred_element_type=F32)
         + jnp.dot(hs.astype(BF16), hm_ref[0], preferred_element_type=F32))
    for t in range(chunk):
        y_ref[pl.ds(t, rows, stride=chunk), :] = y[:, t * LANES:(t + 1) * LANES]


def _s5_conv(u, h0r, h0i, bd, gm, hm, ar, ai, b, t, chunk):
    n, d = u.shape
    n_unit = d // LANES
    sw = ar.shape[-1]
    n_chunk = t // chunk
    lw = chunk * LANES
    if n_chunk == 1:
        hspec = pl.BlockSpec((b, sw), lambda q: (0, q))
        hshape = jax.ShapeDtypeStruct((b, n_unit * sw), F32)
    else:
        h0r, h0i = h0r[:, None, :], h0i[:, None, :]
        hspec = pl.BlockSpec((b, 1, sw), lambda q: (0, 0, q))
        hshape = jax.ShapeDtypeStruct((b, 1, n_unit * sw), F32)
    rows = b * n_chunk
    tok = pl.BlockSpec((n, LANES), lambda q: (0, q))
    y, he_r, he_i = pl.pallas_call(
        functools.partial(_s5_kernel, chunk=chunk, n_chunk=n_chunk),
        grid=(n_unit,),
        in_specs=[
            tok, hspec, hspec,
            pl.BlockSpec((1, chunk, LANES, LANES), lambda q: (q, 0, 0, 0)),
            pl.BlockSpec((1, lw, 2 * sw), lambda q: (q, S5_CHUNK // chunk - 1, 0)),
            pl.BlockSpec((1, 2 * sw, lw), lambda q: (q, 0, 0)),
            pl.BlockSpec((1, 1, sw), lambda q: (q, 0, 0)),
            pl.BlockSpec((1, 1, sw), lambda q: (q, 0, 0)),
        ],
        out_specs=[tok, hspec, hspec],
        out_shape=[jax.ShapeDtypeStruct((n, d), F32), hshape, hshape],
        scratch_shapes=[pltpu.VMEM((lw, lw), BF16), pltpu.VMEM((rows, 2 * sw), F32)],
        compiler_params=_params("parallel"),
        name="s5_conv",
    )(u, h0r, h0i, bd, gm, hm, ar, ai)
    return y, he_r.reshape(b, -1), he_i.reshape(b, -1)


def _s5_out_kernel(x_ref, u_ref, yc_ref, d_ref, w_ref, b_ref, o_ref):
    z = jax.nn.gelu(yc_ref[...] + d_ref[...] * u_ref[...])
    o_ref[...] = x_ref[...] + z * jax.nn.sigmoid(_bdot(z, w_ref[...]) + b_ref[...])


def _s5_out(x, u, yc, d, w, b):
    n, dm = x.shape
    tm = _row_tile(n, 256)
    row = lambda i: (0, 0)
    tile = pl.BlockSpec((tm, dm), lambda i: (i, 0))
    vec = pl.BlockSpec((1, dm), row)
    return pl.pallas_call(
        _s5_out_kernel,
        grid=(n // tm,),
        in_specs=[tile, tile, tile, vec, pl.BlockSpec((dm, dm), row), vec],
        out_specs=tile,
        out_shape=jax.ShapeDtypeStruct((n, dm), F32),
        compiler_params=_params("parallel"),
        name="s5_out",
    )(x, u, yc, d, w, b)


def _block_diag_tiles(w, per_tile):
    n_blk, h, _ = w.shape
    n_tile = n_blk // per_tile
    w = w.reshape(n_tile, per_tile, h, h)
    eye = jnp.eye(per_tile, dtype=w.dtype)
    out = jnp.einsum("tphk,pq->tphqk", w, eye)
    return out.reshape(n_tile, per_tile * h, per_tile * h)


def _head_sum_mats(wa):
    onehot = (jnp.arange(wa)[:, None] // HEAD == jnp.arange(LANES)[None, :])
    return onehot.astype(BF16), onehot.T.astype(BF16)


def _wkv_on_lanes(vecs, v, s0):
    b, t, wa = v.shape
    h = wa // HEAD
    t8 = t // SUBLANES
    tt = min(t, 64)
    if b * h <= LANES:
        rep = LANES // (b * h)
        ip = HEAD // rep
        tp = min(t, LANES)
        keys = [_to_lanes(x, rep, False, tp)[None] for x in vecs]
        v2 = _to_lanes(v, rep, True, tp).reshape(1, t, ip, LANES)
        s2 = s0.reshape(b, h, rep, ip, HEAD).transpose(4, 3, 2, 0, 1).reshape(1, HEAD, ip, LANES)
        y2, s2 = _wkv(keys, v2, s2, tt)
        y = _from_lanes(y2.reshape(t * ip, LANES), b, wa, rep, tp)
        s = s2.reshape(HEAD, ip, rep, b, h).transpose(3, 4, 2, 1, 0).reshape(b, h, HEAD, HEAD)
    else:
        nb = b // LANES
        lay = lambda x: x.reshape(nb, LANES, t8, SUBLANES, h, HEAD).transpose(4, 0, 5, 2, 3, 1).reshape(
            h * nb, HEAD, t8, SUBLANES, LANES)
        v2 = v.reshape(nb, LANES, t, h, HEAD).transpose(3, 0, 2, 4, 1).reshape(h * nb, t, HEAD, LANES)
        s2 = s0.reshape(nb, LANES, h, HEAD, HEAD).transpose(2, 0, 4, 3, 1).reshape(h * nb, HEAD, HEAD, LANES)
        y2, s2 = _wkv([lay(x) for x in vecs], v2, s2, tt)
        y = y2.reshape(h, nb, t, HEAD, LANES).transpose(1, 4, 2, 0, 3).reshape(b, t, wa)
        s = s2.reshape(h, nb, HEAD, HEAD, LANES).transpose(1, 4, 0, 3, 2).reshape(b, h, HEAD, HEAD)
    return y, s


def _seq_blocks(b, t):
    if t >= 256:
        return 1, 256
    return min(b, max(1, 256 // t)), t


def _prepare(w):
    depth = w["norm_ffn1"].shape[0]
    wa = w["w0_a"].shape[1]
    wb = w["lam_b"].shape[1]
    cols_a = w["mu_a"].shape[1]
    ca = ((cols_a + 511) // 512) * 512
    row = lambda v: v.reshape(1, -1)
    bf = lambda v: v.astype(BF16)
    pad_a = lambda v: jnp.pad(v, [(0, 0)] * (v.ndim - 1) + [(0, ca - cols_a)])
    layers = []
    for l in range(depth):
        j = l // 2
        q = dict(ple_gate=bf(w["ple_gate"][l]), ple_proj=bf(w["ple_proj"][l]))
        if l % 2 == 0:
            w_in = w["w_in_ab"][j]
            n_q = wb // (2 * LANES)
            w_out = bf(w["w_out_ab"][j])
            q.update(
                w_in_a=bf(pad_a(w_in[:, :cols_a])), w_in_b=bf(w_in[:, cols_a:]), mu=row(pad_a(w["mu_a"][j])),
                w2=bf(jnp.zeros((LANES, wa), F32).at[:LORA_W].set(w["w2_a"][j])),
                a2=bf(jnp.zeros((LANES, wa), F32).at[LORA_W:LORA_W + LORA_A].set(w["a2_a"][j])),
                g2=bf(jnp.zeros((2 * LANES, wa), F32).at[:LORA_G].set(w["g2_a"][j])),
                wa4=bf(_block_diag_tiles(w["wa_b"][j], (wb // HEAD) // n_q)),
                wx4=bf(_block_diag_tiles(w["wx_b"][j], (wb // HEAD) // n_q)),
                w_out_a=w_out[:wa], w_out_b=w_out[wa:])
        else:
            q.update(
                tabs=_s5_setup(w["a_re_c"][j], w["a_im_c"][j], w["log_dt_c"][j][:, None],
                               w["b_re_c"][j].transpose(0, 2, 1), w["b_im_c"][j].transpose(0, 2, 1),
                               w["c_re_c"][j], w["c_im_c"][j]),
                w_glu=bf(w["w_glu_c"][j]))
        layers.append(q)
    return layers


def _trunk(x, p, st_wkv, st_shift, st_h, st_conv, st_cre, st_cim, w, wp):
    b, t, d = x.shape
    n = b * t
    depth = w["norm_ffn1"].shape[0]
    wa = w["w0_a"].shape[1]
    wb = w["lam_b"].shape[1]
    cols_a = w["mu_a"].shape[1]
    ca = wp[0]["mu"].shape[1]
    n_head = wa // HEAD
    n_grp = w["a_re_c"].shape[1]
    chunk = min(S5_CHUNK, t)
    ered, eexp = _head_sum_mats(wa)
    row = lambda v: v.reshape(1, -1)

    x = x.reshape(n, d)
    new = {k: [] for k in ("wkv", "shift", "h", "conv", "cre", "cim")}
    for l in range(depth):
        j = l // 2
        q = wp[l]
        x = _ffn(x, row(w["norm_ffn1"][l]), w["ffn1_wg"], w["ffn1_wu"], w["ffn1_wd"], l)
        if l % 2 == 0:
            za, zb = _inproj(x, row(w["norm_mix"][l]), q["w_in_a"], q["w_in_b"])
            z3 = za.reshape(b, t, ca)
            zb3 = zb.reshape(b, t, 2 * wb)
            shift = jnp.pad(st_shift[j], ((0, 0), (0, ca - cols_a)))[:, None, :]
            bb, tt = _seq_blocks(b, t)
            r, dec, k2, v, nkk, kka, g = _rwkv_prep(
                z3, shift, q["mu"], row(w["w0_a"][j]), q["w2"], row(w["a0_a"][j]), q["a2"], q["g2"],
                row(w["kk_a"][j]), row(w["ka_a"][j]), ered, eexp, wa, ca, bb, tt)
            y, s_new = _wkv_on_lanes([nkk, dec, kka, k2, r], v, st_wkv[j])
            hist0 = jnp.pad(st_conv[j], ((0, 0), (SUBLANES - (CONV_W - 1), 0), (0, 0)))
            yb, h_new = _rglru(
                zb3, st_h[j][:, None, :], hist0, w["conv_w_b"][j], row(w["conv_b_b"][j]), q["wa4"],
                row(w["ba_b"][j]), q["wx4"], row(w["bx_b"][j]), row(w["lam_b"][j]), bb, tt)
            f2 = lambda a: a.reshape(n, -1)
            x = _mix_out(x, f2(y), f2(r), f2(k2), f2(v), f2(g), f2(yb), row(w["rk_a"][j]),
                         row(w["lnx_g"][j]), row(w["lnx_b"][j]), ered, eexp, q["w_out_a"], q["w_out_b"])
            new["wkv"].append(s_new)
            new["shift"].append(z3[:, t - 1, :cols_a])
            new["h"].append(h_new[:, 0, :])
            conv_all = jnp.concatenate([st_conv[j], zb3[:, max(t - (CONV_W - 1), 0):, :wb]], axis=1)
            new["conv"].append(conv_all[:, conv_all.shape[1] - (CONV_W - 1):])
        else:
            u = _rmsnorm(x, row(w["norm_mix"][l]))
            bd, gm, hm, p_re, p_im = q["tabs"]
            ar = p_re[:, chunk].reshape(bd.shape[0], 1, -1)
            ai = p_im[:, chunk].reshape(bd.shape[0], 1, -1)
            yc, he_r, he_i = _s5_conv(u, st_cre[j].reshape(b, -1), st_cim[j].reshape(b, -1),
                                      bd, gm, hm, ar, ai, b, t, chunk)
            x = _s5_out(x, u, yc, row(w["d_c"][j]), q["w_glu"], row(w["b_glu_c"][j]))
            new["cre"].append(he_r.reshape(b, n_grp, P_C))
            new["cim"].append(he_i.reshape(b, n_grp, P_C))
        x = _ffn(x, row(w["norm_ffn2"][l]), w["ffn2_wg"], w["ffn2_wu"], w["ffn2_wd"], l)
        x = _ple(x, row(w["norm_ple"][l]), q["ple_gate"], p[l].reshape(n, -1), q["ple_proj"],
                 row(w["final_norm"]) if l == depth - 1 else None)
    y = x.reshape(b, t, d)
    stk = lambda name, ref: jnp.stack(new[name]).astype(ref.dtype)
    return y, (stk("wkv", st_wkv), stk("shift", st_shift), stk("h", st_h),
               stk("conv", st_conv), stk("cre", st_cre), stk("cim", st_cim))


def kernel(x_prompt, x_sample, state_a_wkv, state_a_shift, state_b_h, state_b_conv, state_c_re, state_c_im, p_prompt, p_sample, norm_ffn1, ffn1_wg, ffn1_wu, ffn1_wd, norm_mix, norm_ffn2, ffn2_wg, ffn2_wu, ffn2_wd, norm_ple, ple_gate, ple_proj, w_in_ab, mu_a, w0_a, w2_a, a0_a, a2_a, g2_a, kk_a, ka_a, rk_a, lnx_g, lnx_b, conv_w_b, conv_b_b, wa_b, ba_b, wx_b, bx_b, lam_b, w_out_ab, a_re_c, a_im_c, log_dt_c, b_re_c, b_im_c, c_re_c, c_im_c, d_c, w_glu_c, b_glu_c, final_norm):
    w = dict(norm_ffn1=norm_ffn1, ffn1_wg=ffn1_wg, ffn1_wu=ffn1_wu, ffn1_wd=ffn1_wd,
             norm_mix=norm_mix, norm_ffn2=norm_ffn2, ffn2_wg=ffn2_wg, ffn2_wu=ffn2_wu,
             ffn2_wd=ffn2_wd, norm_ple=norm_ple, ple_gate=ple_gate, ple_proj=ple_proj,
             w_in_ab=w_in_ab, mu_a=mu_a, w0_a=w0_a, w2_a=w2_a, a0_a=a0_a, a2_a=a2_a, g2_a=g2_a,
             kk_a=kk_a, ka_a=ka_a, rk_a=rk_a.reshape(rk_a.shape[0], -1), lnx_g=lnx_g, lnx_b=lnx_b,
             conv_w_b=conv_w_b, conv_b_b=conv_b_b, wa_b=wa_b, ba_b=ba_b, wx_b=wx_b, bx_b=bx_b,
             lam_b=lam_b, w_out_ab=w_out_ab,
             a_re_c=a_re_c, a_im_c=a_im_c, log_dt_c=log_dt_c, b_re_c=b_re_c, b_im_c=b_im_c,
             c_re_c=c_re_c, c_im_c=c_im_c, d_c=d_c, w_glu_c=w_glu_c, b_glu_c=b_glu_c,
             final_norm=final_norm)
    wp = _prepare(w)
    bp = x_prompt.shape[0]
    zeros = lambda s: jnp.zeros((s.shape[0], bp) + s.shape[2:], s.dtype)
    y_prompt, prompt_state = _trunk(
        x_prompt, p_prompt, zeros(state_a_wkv), zeros(state_a_shift), zeros(state_b_h),
        zeros(state_b_conv), zeros(state_c_re), zeros(state_c_im), w, wp)
    y_sample, sample_state = _trunk(
        x_sample, p_sample, state_a_wkv, state_a_shift, state_b_h, state_b_conv,
        state_c_re, state_c_im, w, wp)
    return (y_prompt, y_sample) + tuple(prompt_state) + tuple(sample_state)
```

```python
import functools

import jax
import jax.numpy as jnp
from jax import lax
from jax.experimental import pallas as pl
from jax.experimental.pallas import tpu as pltpu

F32 = jnp.float32
BF16 = jnp.bfloat16

HEAD = 64
LANES = 128
SUBLANES = 8
LORA_W = 64
LORA_A = 64
LORA_G = 160
CONV_W = 4
LRU_C = 8.0
GRP_C = 16
P_C = 64
RMS_EPS = 1e-6
GN_EPS = 64e-5
S5_CHUNK = 16
N_ACC = 4
KEY_VECS = ("nkk", "w", "kka", "k", "r")
K_NKK, K_W, K_KKA, K_K, K_R = range(len(KEY_VECS))
VMEM_LIMIT = 56 * 1024 * 1024


def _params(*sem):
    return pltpu.CompilerParams(dimension_semantics=sem, vmem_limit_bytes=VMEM_LIMIT)


def _rms(x, g):
    return x * lax.rsqrt(jnp.mean(x * x, axis=-1, keepdims=True) + RMS_EPS) * g


def _softplus(x):
    return jnp.maximum(x, 0.0) + jnp.log1p(jnp.exp(-jnp.abs(x)))


def _bdot(a, b):
    return jnp.dot(a.astype(BF16), b, preferred_element_type=F32)


def _split_dot(x, e):
    hi = x.astype(BF16)
    lo = (x - hi.astype(F32)).astype(BF16)
    return (jnp.dot(hi, e, preferred_element_type=F32)
            + jnp.dot(lo, e, preferred_element_type=F32))


def _head_sum(x, e_red, e_exp):
    return _split_dot(_split_dot(x, e_red), e_exp)


def _row_tile(n, pref):
    t = min(n, pref)
    while n % t:
        t //= 2
    return t


def _ffn_kernel(x_hbm, g_ref, wg_ref, wu_ref, wd_ref, o_ref, h_ref, x_buf, x_sem):
    i = pl.program_id(0)
    j = pl.program_id(1)
    tm = x_buf.shape[0]

    def x_copy(tile):
        return pltpu.make_async_copy(x_hbm.at[pl.ds(tile * tm, tm)], x_buf, x_sem)

    @pl.when(jnp.logical_and(i == 0, j == 0))
    def _():
        x_copy(0).start()

    @pl.when(j == 0)
    def _():
        x_copy(i).wait()
        x = x_buf[...]
        h_ref[...] = _rms(x, g_ref[...]).astype(BF16)
        o_ref[...] = x

        @pl.when(i + 1 < pl.num_programs(0))
        def _():
            x_copy(i + 1).start()

    h = h_ref[...]
    a = jnp.dot(h, wg_ref[...].astype(BF16), preferred_element_type=F32)
    b = jnp.dot(h, wu_ref[...].astype(BF16), preferred_element_type=F32)
    hid = (0.5 * a * jax.nn.sigmoid(a)) * b
    o_ref[...] += jnp.dot(hid.astype(BF16), wd_ref[...].astype(BF16), preferred_element_type=F32)


def _ffn(x, g, wg, wu, wd, l):
    n, d = x.shape
    f = wg.shape[2]
    tm = _row_tile(n, 1024)
    tf = _row_tile(f, 256)
    return pl.pallas_call(
        _ffn_kernel,
        grid=(n // tm, f // tf),
        in_specs=[
            pl.BlockSpec(memory_space=pl.ANY),
            pl.BlockSpec((1, d), lambda i, j: (0, 0)),
            pl.BlockSpec((None, d, tf), lambda i, j: (l, 0, j)),
            pl.BlockSpec((None, d, tf), lambda i, j: (l, 0, j)),
            pl.BlockSpec((None, tf, d), lambda i, j: (l, j, 0)),
        ],
        out_specs=pl.BlockSpec((tm, d), lambda i, j: (i, 0)),
        out_shape=jax.ShapeDtypeStruct((n, d), F32),
        scratch_shapes=[pltpu.VMEM((tm, d), BF16), pltpu.VMEM((tm, d), F32), pltpu.SemaphoreType.DMA(())],
        compiler_params=_params("arbitrary", "arbitrary"),
        name="ffn",
    )(x, g, wg, wu, wd)


def _ple_kernel(x_ref, g_ref, wgate_ref, p_ref, wproj_ref, *rest):
    o_ref = rest[-1]
    x = x_ref[...]
    gate = jax.nn.sigmoid(_bdot(_rms(x, g_ref[...]), wgate_ref[...]))
    y = x + gate * _bdot(p_ref[...], wproj_ref[...])
    o_ref[...] = _rms(y, rest[0][...]) if len(rest) == 2 else y


def _ple(x, g, wgate, p, wproj, final_g=None):
    n, d = x.shape
    dp = p.shape[1]
    tm = _row_tile(n, 512)
    vec = pl.BlockSpec((1, d), lambda i: (0, 0))
    extra = [] if final_g is None else [final_g]
    return pl.pallas_call(
        _ple_kernel,
        grid=(n // tm,),
        in_specs=[
            pl.BlockSpec((tm, d), lambda i: (i, 0)),
            vec,
            pl.BlockSpec((d, d), lambda i: (0, 0)),
            pl.BlockSpec((tm, dp), lambda i: (i, 0)),
            pl.BlockSpec((dp, d), lambda i: (0, 0)),
        ] + [vec] * len(extra),
        out_specs=pl.BlockSpec((tm, d), lambda i: (i, 0)),
        out_shape=jax.ShapeDtypeStruct((n, d), F32),
        compiler_params=_params("parallel"),
        name="ple",
    )(x, g, wgate, p, wproj, *extra)


def _rmsnorm_kernel(x_ref, g_ref, o_ref):
    o_ref[...] = _rms(x_ref[...], g_ref[...])


def _rmsnorm(x, g):
    n, d = x.shape
    tm = _row_tile(n, 512)
    return pl.pallas_call(
        _rmsnorm_kernel,
        grid=(n // tm,),
        in_specs=[pl.BlockSpec((tm, d), lambda i: (i, 0)),
                  pl.BlockSpec((1, d), lambda i: (0, 0))],
        out_specs=pl.BlockSpec((tm, d), lambda i: (i, 0)),
        out_shape=jax.ShapeDtypeStruct((n, d), F32),
        compiler_params=_params("parallel"),
        name="rmsnorm",
    )(x, g)


def _inproj_kernel(x_ref, g_ref, wa_ref, wb_ref, za_ref, zb_ref):
    h = _rms(x_ref[...], g_ref[...]).astype(BF16)
    za_ref[...] = jnp.dot(h, wa_ref[...], preferred_element_type=F32)
    zb_ref[...] = jnp.dot(h, wb_ref[...], preferred_element_type=F32)


def _inproj(x, g, w_a, w_b):
    n, d = x.shape
    ca, cb = w_a.shape[1], w_b.shape[1]
    tm = _row_tile(n, 256)
    fixed = lambda shape: pl.BlockSpec(shape, lambda i: (0, 0), pipeline_mode=pl.Buffered(1))
    return pl.pallas_call(
        _inproj_kernel,
        grid=(n // tm,),
        in_specs=[pl.BlockSpec((tm, d), lambda i: (i, 0)), pl.BlockSpec((1, d), lambda i: (0, 0)),
                  fixed((d, ca)), fixed((d, cb))],
        out_specs=[pl.BlockSpec((tm, ca), lambda i: (i, 0)), pl.BlockSpec((tm, cb), lambda i: (i, 0))],
        out_shape=[jax.ShapeDtypeStruct((n, ca), F32), jax.ShapeDtypeStruct((n, cb), F32)],
        compiler_params=_params("parallel"),
        name="inproj",
    )(x, g, w_a, w_b)


def _rwkv_prep_kernel(z_ref, shift_ref, mu_ref, w0_ref, w2_ref, a0_ref, a2_ref, g2_ref,
                      kk_ref, ka_ref, ered_ref, eexp_ref,
                      key_ref, v_ref, g_ref, carry_ref, *, wa):
    tb = pl.program_id(1)
    z = z_ref[...]
    bb, tt, ca = z.shape

    @pl.when(tb == 0)
    def _():
        carry_ref[...] = shift_ref[...]

    t_idx = lax.broadcasted_iota(jnp.int32, z.shape, 1)
    zprev = jnp.where(t_idx == 0, carry_ref[...], pltpu.roll(z, 1, axis=1))
    carry_ref[...] = z[:, tt - 1:tt, :]
    zs = (z + (zprev - z) * mu_ref[...]).reshape(bb * tt, ca)

    r = zs[:, 0:wa]
    k = zs[:, wa:2 * wa]
    v = zs[:, 2 * wa:3 * wa]
    xwa = zs[:, 3 * wa:3 * wa + LANES]
    xg = zs[:, 3 * wa + LANES:3 * wa + 3 * LANES]
    w_log = -_softplus(-(w0_ref[...] + _bdot(jnp.tanh(xwa), w2_ref[...]))) - 0.5
    decay = jnp.exp(-jnp.exp(w_log))
    a = jax.nn.sigmoid(a0_ref[...] + _bdot(xwa, a2_ref[...]))
    g = _bdot(jax.nn.sigmoid(xg), g2_ref[...])
    kk = k * kk_ref[...]
    kn = kk / jnp.maximum(jnp.sqrt(_head_sum(kk * kk, ered_ref[...], eexp_ref[...])), 1e-12)
    k2 = k * (1.0 + (a - 1.0) * ka_ref[...])
    shp = (bb, tt, wa)
    for c, val in enumerate((-kn, decay, kn * a, k2, r)):
        key_ref[c] = val.reshape(shp)
    v_ref[...] = v.reshape(shp)
    g_ref[...] = g.reshape(shp)


def _rwkv_prep(z3, shift, mu, w0, w2p, a0, a2p, g2p, kk, ka, ered, eexp, wa, ca, bb, tt):
    b, t, _ = z3.shape
    row = lambda i, j: (0, 0)
    vec = lambda c: pl.BlockSpec((1, c), row)
    out_spec = pl.BlockSpec((bb, tt, wa), lambda i, j: (i, j, 0))
    out_shape = jax.ShapeDtypeStruct((b, t, wa), F32)
    return pl.pallas_call(
        functools.partial(_rwkv_prep_kernel, wa=wa),
        grid=(b // bb, t // tt),
        in_specs=[
            pl.BlockSpec((bb, tt, ca), lambda i, j: (i, j, 0)),
            pl.BlockSpec((bb, 1, ca), lambda i, j: (i, 0, 0)),
            vec(ca), vec(wa),
            pl.BlockSpec(w2p.shape, row), vec(wa),
            pl.BlockSpec(a2p.shape, row), pl.BlockSpec(g2p.shape, row),
            vec(wa), vec(wa),
            pl.BlockSpec(ered.shape, row), pl.BlockSpec(eexp.shape, row),
        ],
        out_specs=[pl.BlockSpec((len(KEY_VECS), bb, tt, wa), lambda i, j: (0, i, j, 0)), out_spec, out_spec],
        out_shape=[jax.ShapeDtypeStruct((len(KEY_VECS), b, t, wa), F32), out_shape, out_shape],
        scratch_shapes=[pltpu.VMEM((bb, 1, ca), F32)],
        compiler_params=_params("parallel", "arbitrary"),
        name="rwkv_prep",
    )(z3, shift, mu, w0, w2p, a0, a2p, g2p, kk, ka, ered, eexp)


def _to_lanes_kernel(x_ref, o_ref, xt_scr, *, rep, split):
    bsz, tt, wa = x_ref.shape
    n_head = wa // HEAD
    n_rows = HEAD // rep if split else HEAD
    for b in range(bsz):
        xt_scr[b] = x_ref[b].T
    for r in range(n_rows):
        offs = [r + q * n_rows for q in range(rep)] if split else [r] * rep
        m = jnp.concatenate(
            [xt_scr[b, pl.ds(off, n_head, stride=HEAD), :] for off in offs for b in range(bsz)], axis=0).T
        if split:
            o_ref[pl.ds(r, tt, stride=n_rows), :] = m
        else:
            o_ref[r] = m.reshape(tt // SUBLANES, SUBLANES, LANES)


def _to_lanes(x, rep, split, tt):
    nv, b, t, wa = x.shape
    n_rows = HEAD // rep if split else HEAD
    if split:
        out_spec = pl.BlockSpec((None, tt * n_rows, LANES), lambda c, i: (c, i, 0))
        out_shape = jax.ShapeDtypeStruct((nv, t * n_rows, LANES), F32)
    else:
        out_spec = pl.BlockSpec((None, HEAD, tt // SUBLANES, SUBLANES, LANES), lambda c, i: (c, 0, i, 0, 0))
        out_shape = jax.ShapeDtypeStruct((nv, HEAD, t // SUBLANES, SUBLANES, LANES), F32)
    return pl.pallas_call(
        functools.partial(_to_lanes_kernel, rep=rep, split=split),
        grid=(nv, t // tt),
        in_specs=[pl.BlockSpec((None, b, tt, wa), lambda c, i: (c, 0, i, 0))],
        out_specs=out_spec,
        out_shape=out_shape,
        scratch_shapes=[pltpu.VMEM((b, wa, tt), F32)],
        compiler_params=_params("parallel", "parallel"),
        name="to_lanes",
    )(x)


def _from_lanes_kernel(y_ref, o_ref, yt_scr, *, rep):
    bsz, tt, wa = o_ref.shape
    n_head = wa // HEAD
    n_rows = HEAD // rep
    for r in range(n_rows):
        m = y_ref[pl.ds(r, tt, stride=n_rows), :].T
        for q in range(rep):
            for b in range(bsz):
                k0 = (q * bsz + b) * n_head
                yt_scr[b, pl.ds(q * n_rows + r, n_head, stride=HEAD), :] = m[k0:k0 + n_head, :]
    for b in range(bsz):
        o_ref[b] = yt_scr[b].T


def _from_lanes(y2, b, wa, rep, tt):
    n_rows = HEAD // rep
    t = y2.shape[0] // n_rows
    return pl.pallas_call(
        functools.partial(_from_lanes_kernel, rep=rep),
        grid=(t // tt,),
        in_specs=[pl.BlockSpec((tt * n_rows, LANES), lambda i: (i, 0))],
        out_specs=pl.BlockSpec((b, tt, wa), lambda i: (0, i, 0)),
        out_shape=jax.ShapeDtypeStruct((b, t, wa), F32),
        scratch_shapes=[pltpu.VMEM((b, wa, tt), F32)],
        compiler_params=_params("parallel"),
        name="from_lanes",
    )(y2)


def _wkv_kernel(key_ref, v_ref, s0_ref, y_ref, s_ref):
    n_key, n_t8 = key_ref.shape[2], key_ref.shape[3]
    slab = s_ref.shape[2:]

    @pl.when(pl.program_id(1) == 0)
    def _():
        s_ref[...] = s0_ref[...]

    def tree_sum(parts):
        while len(parts) > 1:
            parts = [parts[i] + parts[i + 1] for i in range(0, len(parts), 2)]
        return parts[0]

    def tile_step(t8, carry):
        for i in range(SUBLANES):
            t = t8 * SUBLANES + i
            key_row = lambda c, j: key_ref[0, c, j, t8, i:i + 1, :]
            acc = [jnp.zeros(slab, F32) for _ in range(N_ACC)]
            for j in range(n_key):
                acc[j % N_ACC] = acc[j % N_ACC] + s_ref[0, j] * key_row(K_NKK, j)
            sa = tree_sum(acc)
            v = v_ref[0, t]
            acc = [jnp.zeros(slab, F32) for _ in range(N_ACC)]
            for j in range(n_key):
                s = s_ref[0, j] * key_row(K_W, j) + sa * key_row(K_KKA, j) + v * key_row(K_K, j)
                s_ref[0, j] = s
                acc[j % N_ACC] = acc[j % N_ACC] + s * key_row(K_R, j)
            y_ref[0, t] = tree_sum(acc)
        return carry

    lax.fori_loop(0, n_t8, tile_step, 0)


def _wkv(keys, v, s0, tt):
    u, t, rows, _ = v.shape
    n_vec, n_key = keys.shape[1], s0.shape[1]
    st = pl.BlockSpec((1, n_key, rows, LANES), lambda i, j: (i, 0, 0, 0))
    seq = pl.BlockSpec((1, tt, rows, LANES), lambda i, j: (i, j, 0, 0))
    key = pl.BlockSpec((1, n_vec, n_key, tt // SUBLANES, SUBLANES, LANES), lambda i, j: (i, 0, 0, j, 0, 0))
    return pl.pallas_call(
        _wkv_kernel,
        grid=(u, t // tt),
        in_specs=[key, seq, st],
        out_specs=[seq, st],
        out_shape=[jax.ShapeDtypeStruct(v.shape, F32), jax.ShapeDtypeStruct(s0.shape, F32)],
        compiler_params=_params("parallel", "arbitrary"),
        name="wkv",
    )(keys, v, s0)


def _mix_out_kernel(x_ref, y_ref, r_ref, k_ref, v_ref, g_ref, yb_ref, rk_ref, lng_ref, lnb_ref,
                    ered_ref, eexp_ref, woa_ref, wob_ref, o_ref):
    ered = ered_ref[...]
    eexp = eexp_ref[...]
    y = y_ref[...]
    v = v_ref[...]
    mean = _head_sum(y, ered, eexp) * (1.0 / HEAD)
    d = y - mean
    var = _head_sum(d * d, ered, eexp) * (1.0 / HEAD)
    yn = d * lax.rsqrt(var + GN_EPS) * lng_ref[...] + lnb_ref[...]
    bonus = _head_sum(r_ref[...] * k_ref[...] * rk_ref[...], ered, eexp) * v
    ya = (yn + bonus) * g_ref[...]
    o_ref[...] = x_ref[...] + _bdot(ya, woa_ref[...]) + _bdot(yb_ref[...], wob_ref[...])


def _mix_out(x, y, vecs, v, g, yb, rk, lng, lnb, ered, eexp, woa, wob):
    n, d = x.shape
    wa = y.shape[1]
    wb = yb.shape[1]
    tm = _row_tile(n, 256)
    row = lambda i: (0, 0)
    ta = pl.BlockSpec((tm, wa), lambda i: (i, 0))
    va = pl.BlockSpec((1, wa), row)
    vec_tile = lambda c: pl.BlockSpec((None, tm, wa), lambda i: (c, i, 0))
    return pl.pallas_call(
        _mix_out_kernel,
        grid=(n // tm,),
        in_specs=[pl.BlockSpec((tm, d), lambda i: (i, 0)), ta, vec_tile(K_R), vec_tile(K_K), ta, ta,
                  pl.BlockSpec((tm, wb), lambda i: (i, 0)), va, va, va,
                  pl.BlockSpec(ered.shape, row), pl.BlockSpec(eexp.shape, row),
                  pl.BlockSpec(woa.shape, row), pl.BlockSpec(wob.shape, row)],
        out_specs=pl.BlockSpec((tm, d), lambda i: (i, 0)),
        out_shape=jax.ShapeDtypeStruct((n, d), F32),
        compiler_params=_params("parallel"),
        name="mix_out",
    )(x, y, vecs, vecs, v, g, yb, rk, lng, lnb, ered, eexp, woa, wob)


def _rglru_kernel(xb_ref, gb_ref, h0_ref, hist0_ref, cw_ref, cb_ref, wa_ref, ba_ref, wx_ref, bx_ref,
                  lam_ref, y_ref, hl_ref, hist_ref, h_ref, a_scr, b_scr):
    tb = pl.program_id(1)
    x = xb_ref[...]
    bb, tt, wb = x.shape
    n_tile = tt // SUBLANES

    @pl.when(tb == 0)
    def _():
        hist_ref[...] = hist0_ref[...]
        h_ref[...] = h0_ref[...]

    hist = hist_ref[...]
    hist_ref[...] = x[:, tt - SUBLANES:tt, :]
    cw = cw_ref[...]
    t8 = lax.broadcasted_iota(jnp.int32, (bb, SUBLANES, wb), 1)
    xc = cb_ref[...] + x * cw[CONV_W - 1:CONV_W, :]
    for dly in range(1, CONV_W):
        rolled = pltpu.roll(x, dly, axis=1)
        head = jnp.where(t8 < dly, pltpu.roll(hist, dly, axis=1), rolled[:, :SUBLANES, :])
        if n_tile > 1:
            shifted = jnp.concatenate([head, rolled[:, SUBLANES:, :]], axis=1)
        else:
            shifted = head
        xc = xc + shifted * cw[CONV_W - 1 - dly:CONV_W - dly, :]

    xc2 = xc.reshape(bb * tt, wb)
    xcb = xc2.astype(BF16)
    n_q = wa_ref.shape[0]
    wq = wb // n_q
    gr = jnp.concatenate(
        [jnp.dot(xcb[:, q * wq:(q + 1) * wq], wa_ref[q], preferred_element_type=F32) for q in range(n_q)],
        axis=1)
    gi = jnp.concatenate(
        [jnp.dot(xcb[:, q * wq:(q + 1) * wq], wx_ref[q], preferred_element_type=F32) for q in range(n_q)],
        axis=1)
    gate_r = jax.nn.sigmoid(gr + ba_ref[...])
    gate_i = jax.nn.sigmoid(gi + bx_ref[...])
    log_a = (-LRU_C) * gate_r * _softplus(-lam_ref[...])
    a = jnp.exp(log_a)
    b = jnp.sqrt(1.0 - jnp.exp(2.0 * log_a)) * (gate_i * xc2)
    a_scr[...] = a.reshape(bb, tt, wb)
    b_scr[...] = b.reshape(bb, tt, wb)

    def tile_scan(i, h):
        off = pl.multiple_of(i * SUBLANES, SUBLANES)
        at = a_scr[:, pl.ds(off, SUBLANES), :]
        bt = b_scr[:, pl.ds(off, SUBLANES), :]
        for dly in (1, 2, 4):
            keep = t8 >= dly
            bt = jnp.where(keep, at * pltpu.roll(bt, dly, axis=1) + bt, bt)
            at = jnp.where(keep, at * pltpu.roll(at, dly, axis=1), at)
        ht = bt + at * h
        b_scr[:, pl.ds(off, SUBLANES), :] = ht
        return ht[:, SUBLANES - 1:SUBLANES, :]

    h_last = lax.fori_loop(0, n_tile, tile_scan, h_ref[...])
    h_ref[...] = h_last
    hl_ref[...] = h_last
    y_ref[...] = b_scr[...] * jax.nn.gelu(gb_ref[...])


def _rglru(zb3, h0, hist0, cw, cb, wa4, ba, wx4, bx, lam, bb, tt):
    b, t, _ = zb3.shape
    wb = h0.shape[-1]
    row = lambda i, j: (0, 0)
    vec = pl.BlockSpec((1, wb), row)
    st = pl.BlockSpec((bb, 1, wb), lambda i, j: (i, 0, 0))
    return pl.pallas_call(
        _rglru_kernel,
        grid=(b // bb, t // tt),
        in_specs=[
            pl.BlockSpec((bb, tt, wb), lambda i, j: (i, j, 0)),
            pl.BlockSpec((bb, tt, wb), lambda i, j: (i, j, 1)),
            st,
            pl.BlockSpec((bb, SUBLANES, wb), lambda i, j: (i, 0, 0)),
            pl.BlockSpec((CONV_W, wb), row), vec,
            pl.BlockSpec(wa4.shape, lambda i, j: (0, 0, 0)), vec,
            pl.BlockSpec(wx4.shape, lambda i, j: (0, 0, 0)), vec, vec,
        ],
        out_specs=[pl.BlockSpec((bb, tt, wb), lambda i, j: (i, j, 0)), st],
        out_shape=[jax.ShapeDtypeStruct((b, t, wb), F32), jax.ShapeDtypeStruct((b, 1, wb), F32)],
        scratch_shapes=[pltpu.VMEM((bb, SUBLANES, wb), F32), pltpu.VMEM((bb, 1, wb), F32),
                        pltpu.VMEM((bb, tt, wb), F32), pltpu.VMEM((bb, tt, wb), F32)],
        compiler_params=_params("parallel", "arbitrary"),
        name="rglru",
    )(zb3, zb3, h0, hist0, cw, cb, wa4, ba, wx4, bx, lam)


def _s5_setup_kernel(are_ref, aim_ref, ldt_ref, bre_ref, bim_ref, cre_ref, cim_ref,
                     bd_ref, gm_ref, hm_ref, pre_ref, pim_ref):
    a_re = are_ref[...]
    a_im = aim_ref[...]
    dt = jnp.exp(ldt_ref[...])
    mag = jnp.exp(dt * a_re)
    ab_re = mag * jnp.cos(dt * a_im)
    ab_im = mag * jnp.sin(dt * a_im)
    den = a_re * a_re + a_im * a_im
    f_re = ((ab_re - 1.0) * a_re + ab_im * a_im) / den
    f_im = (ab_im * a_re - (ab_re - 1.0) * a_im) / den
    b_re = bre_ref[...]
    b_im = bim_ref[...]
    bb_re = f_re[:, None, :] * b_re - f_im[:, None, :] * b_im
    bb_im = f_re[:, None, :] * b_im + f_im[:, None, :] * b_re
    c_re = cre_ref[...]
    c_im = cim_ref[...]
    rhs = jnp.concatenate([bb_re, bb_im], axis=2)
    gb, n_c, n_p = c_re.shape
    gc = gb * n_c
    sw = gb * n_p
    ii = lambda shape, dim: lax.broadcasted_iota(jnp.int32, shape, dim)
    same = lambda shape, rdiv, ldiv: ii(shape, 0) // rdiv == ii(shape, 1) // ldiv
    tile_p = (ii((n_p, sw), 0) == ii((n_p, sw), 1) % n_p).astype(BF16)
    tile_c = (ii((n_c, gc), 0) == ii((n_c, gc), 1) % n_c).astype(BF16)
    wide = lambda x, tile: jnp.dot(x.astype(BF16), tile, preferred_element_type=F32)
    lane_grp = ii((n_p, gc), 1) // n_c
    p_re = jnp.ones_like(a_re)
    p_im = jnp.zeros_like(a_re)
    for tau in range(S5_CHUNK + 1):
        pre_ref[:, tau, :] = p_re
        pim_ref[:, tau, :] = p_im
        m_re = c_re * p_re[:, None, :] - c_im * p_im[:, None, :]
        m_im = c_re * p_im[:, None, :] + c_im * p_re[:, None, :]
        if tau < S5_CHUNK:
            s = S5_CHUNK - 1 - tau
            pb_re = p_re[:, None, :] * bb_re - p_im[:, None, :] * bb_im
            pb_im = p_re[:, None, :] * bb_im + p_im[:, None, :] * bb_re
            for ri, pb in enumerate((pb_re, pb_im)):
                blk = jnp.where(same((gc, sw), n_c, n_p), wide(pb.reshape(gc, n_p), tile_p), 0.0)
                gm_ref[0, s * gc:(s + 1) * gc, ri * sw:(ri + 1) * sw] = blk.astype(BF16)
            lhs = jnp.concatenate([m_re, -m_im], axis=2)
            k = jnp.einsum("gmk,gnk->gmn", lhs, rhs, precision=lax.Precision.HIGHEST,
                           preferred_element_type=F32)
            kw = jnp.where(same((gc, gc), n_c, n_c), wide(k.reshape(gc, n_c), tile_c), 0.0)
            bd_ref[0, tau] = kw.T.astype(BF16)
        if tau >= 1:
            for ri, m in enumerate((m_re, -m_im)):
                m2 = m.reshape(gc, n_p).astype(BF16).astype(F32)
                mt = jnp.concatenate([m2, jnp.zeros_like(m2)], axis=1).T[:n_p]
                for g in range(gb):
                    hm_ref[0, ri * sw + g * n_p:ri * sw + (g + 1) * n_p, (tau - 1) * gc:tau * gc] = (
                        jnp.where(lane_grp == g, mt, 0.0).astype(BF16))
        p_re, p_im = p_re * ab_re - p_im * ab_im, p_re * ab_im + p_im * ab_re


def _s5_setup(a_re, a_im, log_dt, b_re_t, b_im_t, c_re, c_im):
    g, p = a_re.shape
    c = c_re.shape[1]
    gb = LANES // c
    n_unit = g // gb
    nt = S5_CHUNK + 1
    lw = S5_CHUNK * LANES
    sw = gb * p
    g2 = pl.BlockSpec((gb, p), lambda i: (i, 0))
    g3 = pl.BlockSpec((gb, c, p), lambda i: (i, 0, 0))
    o3 = pl.BlockSpec((gb, nt, p), lambda i: (i, 0, 0))
    s3 = jax.ShapeDtypeStruct((g, nt, p), F32)
    return pl.pallas_call(
        _s5_setup_kernel,
        grid=(n_unit,),
        in_specs=[g2, g2, pl.BlockSpec((gb, 1), lambda i: (i, 0)), g3, g3, g3, g3],
        out_specs=[pl.BlockSpec((1, S5_CHUNK, LANES, LANES), lambda i: (i, 0, 0, 0)),
                   pl.BlockSpec((1, lw, 2 * sw), lambda i: (i, 0, 0)),
                   pl.BlockSpec((1, 2 * sw, lw), lambda i: (i, 0, 0)), o3, o3],
        out_shape=[jax.ShapeDtypeStruct((n_unit, S5_CHUNK, LANES, LANES), BF16),
                   jax.ShapeDtypeStruct((n_unit, lw, 2 * sw), BF16),
                   jax.ShapeDtypeStruct((n_unit, 2 * sw, lw), BF16), s3, s3],
        compiler_params=_params("parallel"),
        name="s5_setup",
    )(a_re, a_im, log_dt, b_re_t, b_im_t, c_re, c_im)


def _s5_kernel(u_ref, h0r_ref, h0i_ref, bd_ref, gm_ref, hm_ref, ar_ref, ai_ref,
               y_ref, her_ref, hei_ref, kt_scr, hs_scr, *, chunk, n_chunk):
    sw = ar_ref.shape[-1]
    rows = u_ref.shape[0] // chunk

    kt_scr[...] = jnp.zeros_like(kt_scr)
    for s in range(chunk):
        for t in range(s, chunk):
            kt_scr[s * LANES:(s + 1) * LANES, t * LANES:(t + 1) * LANES] = bd_ref[0, t - s]

    ucat = jnp.concatenate(
        [u_ref[pl.ds(s, rows, stride=chunk), :] for s in range(chunk)], axis=1).astype(BF16)
    gu = jnp.dot(ucat, gm_ref[0], preferred_element_type=F32)
    a_r = ar_ref[0]
    a_i = ai_ref[0]

    def advance(h_r, h_i, g):
        return a_r * h_r - a_i * h_i + g[:, :sw], a_r * h_i + a_i * h_r + g[:, sw:]

    if n_chunk == 1:
        h_r = h0r_ref[...]
        h_i = h0i_ref[...]
        hs = jnp.concatenate([h_r, h_i], axis=1)
        e_r, e_i = advance(h_r, h_i, gu)
        her_ref[...] = e_r
        hei_ref[...] = e_i
    else:
        hs_scr[...] = gu
        group = SUBLANES if n_chunk % SUBLANES == 0 else n_chunk
        sub = lax.broadcasted_iota(jnp.int32, (group, 2 * sw), 0)
        n_seq = rows // n_chunk

        def tile(it, hs_all):
            out = []
            for b in range(n_seq):
                h_r, h_i = hs_all[b]
                off = pl.multiple_of(b * n_chunk + it * group, group)
                g8 = hs_scr[pl.ds(off, group), :]
                hs8 = jnp.zeros((group, 2 * sw), F32)
                for i in range(group):
                    hs8 = jnp.where(sub == i, jnp.concatenate([h_r, h_i], axis=1), hs8)
                    h_r, h_i = advance(h_r, h_i, g8[i:i + 1, :])
                hs_scr[pl.ds(off, group), :] = hs8
                out.append((h_r, h_i))
            return tuple(out)

        ends = lax.fori_loop(0, n_chunk // group, tile,
                             tuple((h0r_ref[b], h0i_ref[b]) for b in range(n_seq)))
        for b in range(n_seq):
            her_ref[b] = ends[b][0]
            hei_ref[b] = ends[b][1]
        hs = hs_scr[...]
    y = (jnp.dot(ucat, kt_scr[...], preferred_element_type=F32)
         + jnp.dot(hs.astype(BF16), hm_ref[0], preferred_element_type=F32))
    for t in range(chunk):
        y_ref[pl.ds(t, rows, stride=chunk), :] = y[:, t * LANES:(t + 1) * LANES]


def _s5_conv(u, h0r, h0i, bd, gm, hm, ar, ai, b, t, chunk):
    n, d = u.shape
    n_unit = d // LANES
    sw = ar.shape[-1]
    n_chunk = t // chunk
    lw = chunk * LANES
    if n_chunk == 1:
        hspec = pl.BlockSpec((b, sw), lambda q: (0, q))
        hshape = jax.ShapeDtypeStruct((b, n_unit * sw), F32)
    else:
        h0r, h0i = h0r[:, None, :], h0i[:, None, :]
        hspec = pl.BlockSpec((b, 1, sw), lambda q: (0, 0, q))
        hshape = jax.ShapeDtypeStruct((b, 1, n_unit * sw), F32)
    rows = b * n_chunk
    tok = pl.BlockSpec((n, LANES), lambda q: (0, q))
    y, he_r, he_i = pl.pallas_call(
        functools.partial(_s5_kernel, chunk=chunk, n_chunk=n_chunk),
        grid=(n_unit,),
        in_specs=[
            tok, hspec, hspec,
            pl.BlockSpec((1, chunk, LANES, LANES), lambda q: (q, 0, 0, 0)),
            pl.BlockSpec((1, lw, 2 * sw), lambda q: (q, S5_CHUNK // chunk - 1, 0)),
            pl.BlockSpec((1, 2 * sw, lw), lambda q: (q, 0, 0)),
            pl.BlockSpec((1, 1, sw), lambda q: (q, 0, 0)),
            pl.BlockSpec((1, 1, sw), lambda q: (q, 0, 0)),
        ],
        out_specs=[tok, hspec, hspec],
        out_shape=[jax.ShapeDtypeStruct((n, d), F32), hshape, hshape],
        scratch_shapes=[pltpu.VMEM((lw, lw), BF16), pltpu.VMEM((rows, 2 * sw), F32)],
        compiler_params=_params("parallel"),
        name="s5_conv",
    )(u, h0r, h0i, bd, gm, hm, ar, ai)
    return y, he_r.reshape(b, -1), he_i.reshape(b, -1)


def _s5_out_kernel(x_ref, u_ref, yc_ref, d_ref, w_ref, b_ref, o_ref):
    z = jax.nn.gelu(yc_ref[...] + d_ref[...] * u_ref[...])
    o_ref[...] = x_ref[...] + z * jax.nn.sigmoid(_bdot(z, w_ref[...]) + b_ref[...])


def _s5_out(x, u, yc, d, w, b):
    n, dm = x.shape
    tm = _row_tile(n, 256)
    row = lambda i: (0, 0)
    tile = pl.BlockSpec((tm, dm), lambda i: (i, 0))
    vec = pl.BlockSpec((1, dm), row)
    return pl.pallas_call(
        _s5_out_kernel,
        grid=(n // tm,),
        in_specs=[tile, tile, tile, vec, pl.BlockSpec((dm, dm), row), vec],
        out_specs=tile,
        out_shape=jax.ShapeDtypeStruct((n, dm), F32),
        compiler_params=_params("parallel"),
        name="s5_out",
    )(x, u, yc, d, w, b)


def _block_diag_tiles(w, per_tile):
    n_blk, h, _ = w.shape
    n_tile = n_blk // per_tile
    w = w.reshape(n_tile, per_tile, h, h)
    eye = jnp.eye(per_tile, dtype=w.dtype)
    out = jnp.einsum("tphk,pq->tphqk", w, eye)
    return out.reshape(n_tile, per_tile * h, per_tile * h)


def _head_sum_mats(wa):
    onehot = (jnp.arange(wa)[:, None] // HEAD == jnp.arange(LANES)[None, :])
    return onehot.astype(BF16), onehot.T.astype(BF16)


def _wkv_on_lanes(vecs, v, s0):
    b, t, wa = v.shape
    h = wa // HEAD
    t8 = t // SUBLANES
    tt = min(t, 64)
    nv = vecs.shape[0]
    if b * h <= LANES:
        rep = LANES // (b * h)
        ip = HEAD // rep
        tp = min(t, LANES)
        keys = _to_lanes(vecs, rep, False, tp)[None]
        v2 = _to_lanes(v[None], rep, True, tp).reshape(1, t, ip, LANES)
        s2 = s0.reshape(b, h, rep, ip, HEAD).transpose(4, 3, 2, 0, 1).reshape(1, HEAD, ip, LANES)
        y2, s2 = _wkv(keys, v2, s2, tt)
        y = _from_lanes(y2.reshape(t * ip, LANES), b, wa, rep, tp)
        s = s2.reshape(HEAD, ip, rep, b, h).transpose(3, 4, 2, 1, 0).reshape(b, h, HEAD, HEAD)
    else:
        nb = b // LANES
        keys = vecs.reshape(nv, nb, LANES, t8, SUBLANES, h, HEAD).transpose(5, 1, 0, 6, 3, 4, 2).reshape(
            h * nb, nv, HEAD, t8, SUBLANES, LANES)
        v2 = v.reshape(nb, LANES, t, h, HEAD).transpose(3, 0, 2, 4, 1).reshape(h * nb, t, HEAD, LANES)
        s2 = s0.reshape(nb, LANES, h, HEAD, HEAD).transpose(2, 0, 4, 3, 1).reshape(h * nb, HEAD, HEAD, LANES)
        y2, s2 = _wkv(keys, v2, s2, tt)
        y = y2.reshape(h, nb, t, HEAD, LANES).transpose(1, 4, 2, 0, 3).reshape(b, t, wa)
        s = s2.reshape(h, nb, HEAD, HEAD, LANES).transpose(1, 4, 0, 3, 2).reshape(b, h, HEAD, HEAD)
    return y, s


def _seq_blocks(b, t):
    if t >= 256:
        return 1, 256
    return min(b, max(1, 256 // t)), t


def _prepare(w):
    depth = w["norm_ffn1"].shape[0]
    wa = w["w0_a"].shape[1]
    wb = w["lam_b"].shape[1]
    cols_a = w["mu_a"].shape[1]
    ca = ((cols_a + 511) // 512) * 512
    row = lambda v: v.reshape(1, -1)
    bf = lambda v: v.astype(BF16)
    pad_a = lambda v: jnp.pad(v, [(0, 0)] * (v.ndim - 1) + [(0, ca - cols_a)])
    layers = []
    for l in range(depth):
        j = l // 2
        q = dict(ple_gate=bf(w["ple_gate"][l]), ple_proj=bf(w["ple_proj"][l]))
        if l % 2 == 0:
            w_in = w["w_in_ab"][j]
            n_q = wb // (2 * LANES)
            w_out = bf(w["w_out_ab"][j])
            q.update(
                w_in_a=bf(pad_a(w_in[:, :cols_a])), w_in_b=bf(w_in[:, cols_a:]), mu=row(pad_a(w["mu_a"][j])),
                w2=bf(jnp.zeros((LANES, wa), F32).at[:LORA_W].set(w["w2_a"][j])),
                a2=bf(jnp.zeros((LANES, wa), F32).at[LORA_W:LORA_W + LORA_A].set(w["a2_a"][j])),
                g2=bf(jnp.zeros((2 * LANES, wa), F32).at[:LORA_G].set(w["g2_a"][j])),
                wa4=bf(_block_diag_tiles(w["wa_b"][j], (wb // HEAD) // n_q)),
                wx4=bf(_block_diag_tiles(w["wx_b"][j], (wb // HEAD) // n_q)),
                w_out_a=w_out[:wa], w_out_b=w_out[wa:])
        else:
            q.update(
                tabs=_s5_setup(w["a_re_c"][j], w["a_im_c"][j], w["log_dt_c"][j][:, None],
                               w["b_re_c"][j].transpose(0, 2, 1), w["b_im_c"][j].transpose(0, 2, 1),
                               w["c_re_c"][j], w["c_im_c"][j]),
                w_glu=bf(w["w_glu_c"][j]))
        layers.append(q)
    return layers


def _trunk(x, p, st_wkv, st_shift, st_h, st_conv, st_cre, st_cim, w, wp):
    b, t, d = x.shape
    n = b * t
    depth = w["norm_ffn1"].shape[0]
    wa = w["w0_a"].shape[1]
    wb = w["lam_b"].shape[1]
    cols_a = w["mu_a"].shape[1]
    ca = wp[0]["mu"].shape[1]
    n_head = wa // HEAD
    n_grp = w["a_re_c"].shape[1]
    chunk = min(S5_CHUNK, t)
    ered, eexp = _head_sum_mats(wa)
    row = lambda v: v.reshape(1, -1)

    x = x.reshape(n, d)
    new = {k: [] for k in ("wkv", "shift", "h", "conv", "cre", "cim")}
    for l in range(depth):
        j = l // 2
        q = wp[l]
        x = _ffn(x, row(w["norm_ffn1"][l]), w["ffn1_wg"], w["ffn1_wu"], w["ffn1_wd"], l)
        if l % 2 == 0:
            za, zb = _inproj(x, row(w["norm_mix"][l]), q["w_in_a"], q["w_in_b"])
            z3 = za.reshape(b, t, ca)
            zb3 = zb.reshape(b, t, 2 * wb)
            shift = jnp.pad(st_shift[j], ((0, 0), (0, ca - cols_a)))[:, None, :]
            bb, tt = _seq_blocks(b, t)
            vecs, v, g = _rwkv_prep(
                z3, shift, q["mu"], row(w["w0_a"][j]), q["w2"], row(w["a0_a"][j]), q["a2"], q["g2"],
                row(w["kk_a"][j]), row(w["ka_a"][j]), ered, eexp, wa, ca, bb, tt)
            y, s_new = _wkv_on_lanes(vecs, v, st_wkv[j])
            hist0 = jnp.pad(st_conv[j], ((0, 0), (SUBLANES - (CONV_W - 1), 0), (0, 0)))
            yb, h_new = _rglru(
                zb3, st_h[j][:, None, :], hist0, w["conv_w_b"][j], row(w["conv_b_b"][j]), q["wa4"],
                row(w["ba_b"][j]), q["wx4"], row(w["bx_b"][j]), row(w["lam_b"][j]), bb, tt)
            f2 = lambda a: a.reshape(n, -1)
            x = _mix_out(x, f2(y), vecs.reshape(-1, n, wa), f2(v), f2(g), f2(yb), row(w["rk_a"][j]),
                         row(w["lnx_g"][j]), row(w["lnx_b"][j]), ered, eexp, q["w_out_a"], q["w_out_b"])
            new["wkv"].append(s_new)
            new["shift"].append(z3[:, t - 1, :cols_a])
            new["h"].append(h_new[:, 0, :])
            conv_all = jnp.concatenate([st_conv[j], zb3[:, max(t - (CONV_W - 1), 0):, :wb]], axis=1)
            new["conv"].append(conv_all[:, conv_all.shape[1] - (CONV_W - 1):])
        else:
            u = _rmsnorm(x, row(w["norm_mix"][l]))
            bd, gm, hm, p_re, p_im = q["tabs"]
            ar = p_re[:, chunk].reshape(bd.shape[0], 1, -1)
            ai = p_im[:, chunk].reshape(bd.shape[0], 1, -1)
            yc, he_r, he_i = _s5_conv(u, st_cre[j].reshape(b, -1), st_cim[j].reshape(b, -1),
                                      bd, gm, hm, ar, ai, b, t, chunk)
            x = _s5_out(x, u, yc, row(w["d_c"][j]), q["w_glu"], row(w["b_glu_c"][j]))
            new["cre"].append(he_r.reshape(b, n_grp, P_C))
            new["cim"].append(he_i.reshape(b, n_grp, P_C))
        x = _ffn(x, row(w["norm_ffn2"][l]), w["ffn2_wg"], w["ffn2_wu"], w["ffn2_wd"], l)
        x = _ple(x, row(w["norm_ple"][l]), q["ple_gate"], p[l].reshape(n, -1), q["ple_proj"],
                 row(w["final_norm"]) if l == depth - 1 else None)
    y = x.reshape(b, t, d)
    stk = lambda name, ref: jnp.stack(new[name]).astype(ref.dtype)
    return y, (stk("wkv", st_wkv), stk("shift", st_shift), stk("h", st_h),
               stk("conv", st_conv), stk("cre", st_cre), stk("cim", st_cim))


def kernel(x_prompt, x_sample, state_a_wkv, state_a_shift, state_b_h, state_b_conv, state_c_re, state_c_im, p_prompt, p_sample, norm_ffn1, ffn1_wg, ffn1_wu, ffn1_wd, norm_mix, norm_ffn2, ffn2_wg, ffn2_wu, ffn2_wd, norm_ple, ple_gate, ple_proj, w_in_ab, mu_a, w0_a, w2_a, a0_a, a2_a, g2_a, kk_a, ka_a, rk_a, lnx_g, lnx_b, conv_w_b, conv_b_b, wa_b, ba_b, wx_b, bx_b, lam_b, w_out_ab, a_re_c, a_im_c, log_dt_c, b_re_c, b_im_c, c_re_c, c_im_c, d_c, w_glu_c, b_glu_c, final_norm):
    w = dict(norm_ffn1=norm_ffn1, ffn1_wg=ffn1_wg, ffn1_wu=ffn1_wu, ffn1_wd=ffn1_wd,
             norm_mix=norm_mix, norm_ffn2=norm_ffn2, ffn2_wg=ffn2_wg, ffn2_wu=ffn2_wu,
             ffn2_wd=ffn2_wd, norm_ple=norm_ple, ple_gate=ple_gate, ple_proj=ple_proj,
             w_in_ab=w_in_ab, mu_a=mu_a, w0_a=w0_a, w2_a=w2_a, a0_a=a0_a, a2_a=a2_a, g2_a=g2_a,
             kk_a=kk_a, ka_a=ka_a, rk_a=rk_a.reshape(rk_a.shape[0], -1), lnx_g=lnx_g, lnx_b=lnx_b,
             conv_w_b=conv_w_b, conv_b_b=conv_b_b, wa_b=wa_b, ba_b=ba_b, wx_b=wx_b, bx_b=bx_b,
             lam_b=lam_b, w_out_ab=w_out_ab,
             a_re_c=a_re_c, a_im_c=a_im_c, log_dt_c=log_dt_c, b_re_c=b_re_c, b_im_c=b_im_c,
             c_re_c=c_re_c, c_im_c=c_im_c, d_c=d_c, w_glu_c=w_glu_c, b_glu_c=b_glu_c,
             final_norm=final_norm)
    wp = _prepare(w)
    bp = x_prompt.shape[0]
    zeros = lambda s: jnp.zeros((s.shape[0], bp) + s.shape[2:], s.dtype)
    y_prompt, prompt_state = _trunk(
        x_prompt, p_prompt, zeros(state_a_wkv), zeros(state_a_shift), zeros(state_b_h),
        zeros(state_b_conv), zeros(state_c_re), zeros(state_c_im), w, wp)
    y_sample, sample_state = _trunk(
        x_sample, p_sample, state_a_wkv, state_a_shift, state_b_h, state_b_conv,
        state_c_re, state_c_im, w, wp)
    return (y_prompt, y_sample) + tuple(prompt_state) + tuple(sample_state)
```

```python
import functools

import jax
import jax.numpy as jnp
from jax import lax
from jax.experimental import pallas as pl
from jax.experimental.pallas import tpu as pltpu

F32 = jnp.float32
BF16 = jnp.bfloat16

HEAD = 64
LANES = 128
SUBLANES = 8
LORA_W = 64
LORA_A = 64
LORA_G = 160
CONV_W = 4
LRU_C = 8.0
GRP_C = 16
P_C = 64
RMS_EPS = 1e-6
GN_EPS = 64e-5
S5_CHUNK = 16
N_ACC = 4
VMEM_LIMIT = 56 * 1024 * 1024


def _params(*sem):
    return pltpu.CompilerParams(dimension_semantics=sem, vmem_limit_bytes=VMEM_LIMIT)


def _rms(x, g):
    return x * lax.rsqrt(jnp.mean(x * x, axis=-1, keepdims=True) + RMS_EPS) * g


def _softplus(x):
    return jnp.maximum(x, 0.0) + jnp.log1p(jnp.exp(-jnp.abs(x)))


def _bdot(a, b):
    return jnp.dot(a.astype(BF16), b, preferred_element_type=F32)


def _split_dot(x, e):
    hi = x.astype(BF16)
    lo = (x - hi.astype(F32)).astype(BF16)
    return (jnp.dot(hi, e, preferred_element_type=F32)
            + jnp.dot(lo, e, preferred_element_type=F32))


def _head_sum(x, e_red, e_exp):
    return _split_dot(_split_dot(x, e_red), e_exp)


def _row_tile(n, pref):
    t = min(n, pref)
    while n % t:
        t //= 2
    return t


def _ffn_kernel(x_hbm, g_ref, wg_ref, wu_ref, wd_ref, o_ref, h_ref, x_buf, x_sem):
    i = pl.program_id(0)
    j = pl.program_id(1)
    tm = x_buf.shape[0]

    def x_copy(tile):
        return pltpu.make_async_copy(x_hbm.at[pl.ds(tile * tm, tm)], x_buf, x_sem)

    @pl.when(jnp.logical_and(i == 0, j == 0))
    def _():
        x_copy(0).start()

    @pl.when(j == 0)
    def _():
        x_copy(i).wait()
        x = x_buf[...]
        h_ref[...] = _rms(x, g_ref[...]).astype(BF16)
        o_ref[...] = x

        @pl.when(i + 1 < pl.num_programs(0))
        def _():
            x_copy(i + 1).start()

    h = h_ref[...]
    a = jnp.dot(h, wg_ref[...].astype(BF16), preferred_element_type=F32)
    b = jnp.dot(h, wu_ref[...].astype(BF16), preferred_element_type=F32)
    hid = (0.5 * a * jax.nn.sigmoid(a)) * b
    o_ref[...] += jnp.dot(hid.astype(BF16), wd_ref[...].astype(BF16), preferred_element_type=F32)


def _ffn(x, g, wg, wu, wd, l):
    n, d = x.shape
    f = wg.shape[2]
    tm = _row_tile(n, 1024)
    tf = _row_tile(f, 256)
    return pl.pallas_call(
        _ffn_kernel,
        grid=(n // tm, f // tf),
        in_specs=[
            pl.BlockSpec(memory_space=pl.ANY),
            pl.BlockSpec((1, d), lambda i, j: (0, 0)),
            pl.BlockSpec((None, d, tf), lambda i, j: (l, 0, j)),
            pl.BlockSpec((None, d, tf), lambda i, j: (l, 0, j)),
            pl.BlockSpec((None, tf, d), lambda i, j: (l, j, 0)),
        ],
        out_specs=pl.BlockSpec((tm, d), lambda i, j: (i, 0)),
        out_shape=jax.ShapeDtypeStruct((n, d), F32),
        scratch_shapes=[pltpu.VMEM((tm, d), BF16), pltpu.VMEM((tm, d), F32), pltpu.SemaphoreType.DMA(())],
        compiler_params=_params("arbitrary", "arbitrary"),
        name="ffn",
    )(x, g, wg, wu, wd)


def _ple_kernel(x_ref, g_ref, wgate_ref, p_ref, wproj_ref, *rest):
    o_ref = rest[-1]
    x = x_ref[...]
    gate = jax.nn.sigmoid(_bdot(_rms(x, g_ref[...]), wgate_ref[...]))
    y = x + gate * _bdot(p_ref[...], wproj_ref[...])
    o_ref[...] = _rms(y, rest[0][...]) if len(rest) == 2 else y


def _ple(x, g, wgate, p, wproj, final_g=None):
    n, d = x.shape
    dp = p.shape[1]
    tm = _row_tile(n, 512)
    vec = pl.BlockSpec((1, d), lambda i: (0, 0))
    extra = [] if final_g is None else [final_g]
    return pl.pallas_call(
        _ple_kernel,
        grid=(n // tm,),
        in_specs=[
            pl.BlockSpec((tm, d), lambda i: (i, 0)),
            vec,
            pl.BlockSpec((d, d), lambda i: (0, 0)),
            pl.BlockSpec((tm, dp), lambda i: (i, 0)),
            pl.BlockSpec((dp, d), lambda i: (0, 0)),
        ] + [vec] * len(extra),
        out_specs=pl.BlockSpec((tm, d), lambda i: (i, 0)),
        out_shape=jax.ShapeDtypeStruct((n, d), F32),
        compiler_params=_params("parallel"),
        name="ple",
    )(x, g, wgate, p, wproj, *extra)


def _rmsnorm_kernel(x_ref, g_ref, o_ref):
    o_ref[...] = _rms(x_ref[...], g_ref[...])


def _rmsnorm(x, g):
    n, d = x.shape
    tm = _row_tile(n, 512)
    return pl.pallas_call(
        _rmsnorm_kernel,
        grid=(n // tm,),
        in_specs=[pl.BlockSpec((tm, d), lambda i: (i, 0)),
                  pl.BlockSpec((1, d), lambda i: (0, 0))],
        out_specs=pl.BlockSpec((tm, d), lambda i: (i, 0)),
        out_shape=jax.ShapeDtypeStruct((n, d), F32),
        compiler_params=_params("parallel"),
        name="rmsnorm",
    )(x, g)


def _inproj_kernel(x_ref, g_ref, wa_ref, wb_ref, za_ref, zb_ref):
    h = _rms(x_ref[...], g_ref[...]).astype(BF16)
    za_ref[...] = jnp.dot(h, wa_ref[...], preferred_element_type=F32)
    zb_ref[...] = jnp.dot(h, wb_ref[...], preferred_element_type=F32)


def _inproj(x, g, w_a, w_b):
    n, d = x.shape
    ca, cb = w_a.shape[1], w_b.shape[1]
    tm = _row_tile(n, 256)
    fixed = lambda shape: pl.BlockSpec(shape, lambda i: (0, 0), pipeline_mode=pl.Buffered(1))
    return pl.pallas_call(
        _inproj_kernel,
        grid=(n // tm,),
        in_specs=[pl.BlockSpec((tm, d), lambda i: (i, 0)), pl.BlockSpec((1, d), lambda i: (0, 0)),
                  fixed((d, ca)), fixed((d, cb))],
        out_specs=[pl.BlockSpec((tm, ca), lambda i: (i, 0)), pl.BlockSpec((tm, cb), lambda i: (i, 0))],
        out_shape=[jax.ShapeDtypeStruct((n, ca), F32), jax.ShapeDtypeStruct((n, cb), F32)],
        compiler_params=_params("parallel"),
        name="inproj",
    )(x, g, w_a, w_b)


def _rwkv_prep_kernel(z_ref, shift_ref, mu_ref, w0_ref, w2_ref, a0_ref, a2_ref, g2_ref,
                      kk_ref, ka_ref, ered_ref, eexp_ref,
                      r_ref, w_ref, k_ref, v_ref, nkk_ref, kka_ref, g_ref, carry_ref, *, wa):
    tb = pl.program_id(1)
    z = z_ref[...]
    bb, tt, ca = z.shape

    @pl.when(tb == 0)
    def _():
        carry_ref[...] = shift_ref[...]

    t_idx = lax.broadcasted_iota(jnp.int32, z.shape, 1)
    zprev = jnp.where(t_idx == 0, carry_ref[...], pltpu.roll(z, 1, axis=1))
    carry_ref[...] = z[:, tt - 1:tt, :]
    zs = (z + (zprev - z) * mu_ref[...]).reshape(bb * tt, ca)

    r = zs[:, 0:wa]
    k = zs[:, wa:2 * wa]
    v = zs[:, 2 * wa:3 * wa]
    xwa = zs[:, 3 * wa:3 * wa + LANES]
    xg = zs[:, 3 * wa + LANES:3 * wa + 3 * LANES]
    w_log = -_softplus(-(w0_ref[...] + _bdot(jnp.tanh(xwa), w2_ref[...]))) - 0.5
    decay = jnp.exp(-jnp.exp(w_log))
    a = jax.nn.sigmoid(a0_ref[...] + _bdot(xwa, a2_ref[...]))
    g = _bdot(jax.nn.sigmoid(xg), g2_ref[...])
    kk = k * kk_ref[...]
    kn = kk / jnp.maximum(jnp.sqrt(_head_sum(kk * kk, ered_ref[...], eexp_ref[...])), 1e-12)
    k2 = k * (1.0 + (a - 1.0) * ka_ref[...])
    shp = (bb, tt, wa)
    r_ref[...] = r.reshape(shp)
    w_ref[...] = decay.reshape(shp)
    k_ref[...] = k2.reshape(shp)
    v_ref[...] = v.reshape(shp)
    nkk_ref[...] = (-kn).reshape(shp)
    kka_ref[...] = (kn * a).reshape(shp)
    g_ref[...] = g.reshape(shp)


def _rwkv_prep(z3, shift, mu, w0, w2p, a0, a2p, g2p, kk, ka, ered, eexp, wa, ca, bb, tt):
    b, t, _ = z3.shape
    row = lambda i, j: (0, 0)
    vec = lambda c: pl.BlockSpec((1, c), row)
    out_spec = pl.BlockSpec((bb, tt, wa), lambda i, j: (i, j, 0))
    out_shape = jax.ShapeDtypeStruct((b, t, wa), F32)
    return pl.pallas_call(
        functools.partial(_rwkv_prep_kernel, wa=wa),
        grid=(b // bb, t // tt),
        in_specs=[
            pl.BlockSpec((bb, tt, ca), lambda i, j: (i, j, 0)),
            pl.BlockSpec((bb, 1, ca), lambda i, j: (i, 0, 0)),
            vec(ca), vec(wa),
            pl.BlockSpec(w2p.shape, row), vec(wa),
            pl.BlockSpec(a2p.shape, row), pl.BlockSpec(g2p.shape, row),
            vec(wa), vec(wa),
            pl.BlockSpec(ered.shape, row), pl.BlockSpec(eexp.shape, row),
        ],
        out_specs=[out_spec] * 7,
        out_shape=[out_shape] * 7,
        scratch_shapes=[pltpu.VMEM((bb, 1, ca), F32)],
        compiler_params=_params("parallel", "arbitrary"),
        name="rwkv_prep",
    )(z3, shift, mu, w0, w2p, a0, a2p, g2p, kk, ka, ered, eexp)


def _to_lanes_kernel(x_ref, o_ref, xt_scr, *, rep, split):
    bsz, tt, wa = x_ref.shape
    n_head = wa // HEAD
    n_rows = HEAD // rep if split else HEAD
    for b in range(bsz):
        xt_scr[b] = x_ref[b].T
    for r in range(n_rows):
        offs = [r + q * n_rows for q in range(rep)] if split else [r] * rep
        m = jnp.concatenate(
            [xt_scr[b, pl.ds(off, n_head, stride=HEAD), :] for off in offs for b in range(bsz)], axis=0).T
        if split:
            o_ref[pl.ds(r, tt, stride=n_rows), :] = m
        else:
            o_ref[r] = m.reshape(tt // SUBLANES, SUBLANES, LANES)


def _to_lanes(x, rep, split, tt):
    b, t, wa = x.shape
    n_rows = HEAD // rep if split else HEAD
    if split:
        out_spec = pl.BlockSpec((tt * n_rows, LANES), lambda i: (i, 0))
        out_shape = jax.ShapeDtypeStruct((t * n_rows, LANES), F32)
    else:
        out_spec = pl.BlockSpec((HEAD, tt // SUBLANES, SUBLANES, LANES), lambda i: (0, i, 0, 0))
        out_shape = jax.ShapeDtypeStruct((HEAD, t // SUBLANES, SUBLANES, LANES), F32)
    return pl.pallas_call(
        functools.partial(_to_lanes_kernel, rep=rep, split=split),
        grid=(t // tt,),
        in_specs=[pl.BlockSpec((b, tt, wa), lambda i: (0, i, 0))],
        out_specs=out_spec,
        out_shape=out_shape,
        scratch_shapes=[pltpu.VMEM((b, wa, tt), F32)],
        compiler_params=_params("parallel"),
        name="to_lanes",
    )(x)


def _from_lanes_kernel(y_ref, o_ref, yt_scr, *, rep):
    bsz, tt, wa = o_ref.shape
    n_head = wa // HEAD
    n_rows = HEAD // rep
    for r in range(n_rows):
        m = y_ref[pl.ds(r, tt, stride=n_rows), :].T
        for q in range(rep):
            for b in range(bsz):
                k0 = (q * bsz + b) * n_head
                yt_scr[b, pl.ds(q * n_rows + r, n_head, stride=HEAD), :] = m[k0:k0 + n_head, :]
    for b in range(bsz):
        o_ref[b] = yt_scr[b].T


def _from_lanes(y2, b, wa, rep, tt):
    n_rows = HEAD // rep
    t = y2.shape[0] // n_rows
    return pl.pallas_call(
        functools.partial(_from_lanes_kernel, rep=rep),
        grid=(t // tt,),
        in_specs=[pl.BlockSpec((tt * n_rows, LANES), lambda i: (i, 0))],
        out_specs=pl.BlockSpec((b, tt, wa), lambda i: (0, i, 0)),
        out_shape=jax.ShapeDtypeStruct((b, t, wa), F32),
        scratch_shapes=[pltpu.VMEM((b, wa, tt), F32)],
        compiler_params=_params("parallel"),
        name="from_lanes",
    )(y2)


def _wkv_kernel(nkk_ref, w_ref, kka_ref, k_ref, r_ref, v_ref, s0_ref, y_ref, s_ref):
    n_key, n_t8 = nkk_ref.shape[1], nkk_ref.shape[2]
    slab = s_ref.shape[2:]

    @pl.when(pl.program_id(1) == 0)
    def _():
        s_ref[...] = s0_ref[...]

    def tree_sum(parts):
        while len(parts) > 1:
            parts = [parts[i] + parts[i + 1] for i in range(0, len(parts), 2)]
        return parts[0]

    def tile_step(t8, carry):
        for i in range(SUBLANES):
            t = t8 * SUBLANES + i
            key_row = lambda ref, j: ref[0, j, t8, i:i + 1, :]
            acc = [jnp.zeros(slab, F32) for _ in range(N_ACC)]
            for j in range(n_key):
                acc[j % N_ACC] = acc[j % N_ACC] + s_ref[0, j] * key_row(nkk_ref, j)
            sa = tree_sum(acc)
            v = v_ref[0, t]
            acc = [jnp.zeros(slab, F32) for _ in range(N_ACC)]
            for j in range(n_key):
                s = s_ref[0, j] * key_row(w_ref, j) + sa * key_row(kka_ref, j) + v * key_row(k_ref, j)
                s_ref[0, j] = s
                acc[j % N_ACC] = acc[j % N_ACC] + s * key_row(r_ref, j)
            y_ref[0, t] = tree_sum(acc)
        return carry

    lax.fori_loop(0, n_t8, tile_step, 0)


def _wkv(keys, v, s0, tt):
    u, t, rows, _ = v.shape
    n_key = s0.shape[1]
    st = pl.BlockSpec((1, n_key, rows, LANES), lambda i, j: (i, 0, 0, 0))
    seq = pl.BlockSpec((1, tt, rows, LANES), lambda i, j: (i, j, 0, 0))
    key = pl.BlockSpec((1, n_key, tt // SUBLANES, SUBLANES, LANES), lambda i, j: (i, 0, j, 0, 0))
    return pl.pallas_call(
        _wkv_kernel,
        grid=(u, t // tt),
        in_specs=[key] * 5 + [seq, st],
        out_specs=[seq, st],
        out_shape=[jax.ShapeDtypeStruct(v.shape, F32), jax.ShapeDtypeStruct(s0.shape, F32)],
        compiler_params=_params("parallel", "arbitrary"),
        name="wkv",
    )(*keys, v, s0)


def _mix_out_kernel(x_ref, y_ref, r_ref, k_ref, v_ref, g_ref, yb_ref, rk_ref, lng_ref, lnb_ref,
                    ered_ref, eexp_ref, woa_ref, wob_ref, o_ref):
    ered = ered_ref[...]
    eexp = eexp_ref[...]
    y = y_ref[...]
    v = v_ref[...]
    mean = _head_sum(y, ered, eexp) * (1.0 / HEAD)
    d = y - mean
    var = _head_sum(d * d, ered, eexp) * (1.0 / HEAD)
    yn = d * lax.rsqrt(var + GN_EPS) * lng_ref[...] + lnb_ref[...]
    bonus = _head_sum(r_ref[...] * k_ref[...] * rk_ref[...], ered, eexp) * v
    ya = (yn + bonus) * g_ref[...]
    o_ref[...] = x_ref[...] + _bdot(ya, woa_ref[...]) + _bdot(yb_ref[...], wob_ref[...])


def _mix_out(x, y, r, k, v, g, yb, rk, lng, lnb, ered, eexp, woa, wob):
    n, d = x.shape
    wa = y.shape[1]
    wb = yb.shape[1]
    tm = _row_tile(n, 256)
    row = lambda i: (0, 0)
    ta = pl.BlockSpec((tm, wa), lambda i: (i, 0))
    va = pl.BlockSpec((1, wa), row)
    return pl.pallas_call(
        _mix_out_kernel,
        grid=(n // tm,),
        in_specs=[pl.BlockSpec((tm, d), lambda i: (i, 0)), ta, ta, ta, ta, ta,
                  pl.BlockSpec((tm, wb), lambda i: (i, 0)), va, va, va,
                  pl.BlockSpec(ered.shape, row), pl.BlockSpec(eexp.shape, row),
                  pl.BlockSpec(woa.shape, row), pl.BlockSpec(wob.shape, row)],
        out_specs=pl.BlockSpec((tm, d), lambda i: (i, 0)),
        out_shape=jax.ShapeDtypeStruct((n, d), F32),
        compiler_params=_params("parallel"),
        name="mix_out",
    )(x, y, r, k, v, g, yb, rk, lng, lnb, ered, eexp, woa, wob)


def _rglru_kernel(xb_ref, gb_ref, h0_ref, hist0_ref, cw_ref, cb_ref, wa_ref, ba_ref, wx_ref, bx_ref,
                  lam_ref, y_ref, hl_ref, hist_ref, h_ref, a_scr, b_scr):
    tb = pl.program_id(1)
    x = xb_ref[...]
    bb, tt, wb = x.shape
    n_tile = tt // SUBLANES

    @pl.when(tb == 0)
    def _():
        hist_ref[...] = hist0_ref[...]
        h_ref[...] = h0_ref[...]

    hist = hist_ref[...]
    hist_ref[...] = x[:, tt - SUBLANES:tt, :]
    cw = cw_ref[...]
    t8 = lax.broadcasted_iota(jnp.int32, (bb, SUBLANES, wb), 1)
    xc = cb_ref[...] + x * cw[CONV_W - 1:CONV_W, :]
    for dly in range(1, CONV_W):
        rolled = pltpu.roll(x, dly, axis=1)
        head = jnp.where(t8 < dly, pltpu.roll(hist, dly, axis=1), rolled[:, :SUBLANES, :])
        if n_tile > 1:
            shifted = jnp.concatenate([head, rolled[:, SUBLANES:, :]], axis=1)
        else:
            shifted = head
        xc = xc + shifted * cw[CONV_W - 1 - dly:CONV_W - dly, :]

    xc2 = xc.reshape(bb * tt, wb)
    xcb = xc2.astype(BF16)
    n_q = wa_ref.shape[0]
    wq = wb // n_q
    gr = jnp.concatenate(
        [jnp.dot(xcb[:, q * wq:(q + 1) * wq], wa_ref[q], preferred_element_type=F32) for q in range(n_q)],
        axis=1)
    gi = jnp.concatenate(
        [jnp.dot(xcb[:, q * wq:(q + 1) * wq], wx_ref[q], preferred_element_type=F32) for q in range(n_q)],
        axis=1)
    gate_r = jax.nn.sigmoid(gr + ba_ref[...])
    gate_i = jax.nn.sigmoid(gi + bx_ref[...])
    log_a = (-LRU_C) * gate_r * _softplus(-lam_ref[...])
    a = jnp.exp(log_a)
    b = jnp.sqrt(1.0 - jnp.exp(2.0 * log_a)) * (gate_i * xc2)
    a_scr[...] = a.reshape(bb, tt, wb)
    b_scr[...] = b.reshape(bb, tt, wb)

    def tile_scan(i, h):
        off = pl.multiple_of(i * SUBLANES, SUBLANES)
        at = a_scr[:, pl.ds(off, SUBLANES), :]
        bt = b_scr[:, pl.ds(off, SUBLANES), :]
        for dly in (1, 2, 4):
            keep = t8 >= dly
            bt = jnp.where(keep, at * pltpu.roll(bt, dly, axis=1) + bt, bt)
            at = jnp.where(keep, at * pltpu.roll(at, dly, axis=1), at)
        ht = bt + at * h
        b_scr[:, pl.ds(off, SUBLANES), :] = ht
        return ht[:, SUBLANES - 1:SUBLANES, :]

    h_last = lax.fori_loop(0, n_tile, tile_scan, h_ref[...])
    h_ref[...] = h_last
    hl_ref[...] = h_last
    y_ref[...] = b_scr[...] * jax.nn.gelu(gb_ref[...])


def _rglru(zb3, h0, hist0, cw, cb, wa4, ba, wx4, bx, lam, bb, tt):
    b, t, _ = zb3.shape
    wb = h0.shape[-1]
    row = lambda i, j: (0, 0)
    vec = pl.BlockSpec((1, wb), row)
    st = pl.BlockSpec((bb, 1, wb), lambda i, j: (i, 0, 0))
    return pl.pallas_call(
        _rglru_kernel,
        grid=(b // bb, t // tt),
        in_specs=[
            pl.BlockSpec((bb, tt, wb), lambda i, j: (i, j, 0)),
            pl.BlockSpec((bb, tt, wb), lambda i, j: (i, j, 1)),
            st,
            pl.BlockSpec((bb, SUBLANES, wb), lambda i, j: (i, 0, 0)),
            pl.BlockSpec((CONV_W, wb), row), vec,
            pl.BlockSpec(wa4.shape, lambda i, j: (0, 0, 0)), vec,
            pl.BlockSpec(wx4.shape, lambda i, j: (0, 0, 0)), vec, vec,
        ],
        out_specs=[pl.BlockSpec((bb, tt, wb), lambda i, j: (i, j, 0)), st],
        out_shape=[jax.ShapeDtypeStruct((b, t, wb), F32), jax.ShapeDtypeStruct((b, 1, wb), F32)],
        scratch_shapes=[pltpu.VMEM((bb, SUBLANES, wb), F32), pltpu.VMEM((bb, 1, wb), F32),
                        pltpu.VMEM((bb, tt, wb), F32), pltpu.VMEM((bb, tt, wb), F32)],
        compiler_params=_params("parallel", "arbitrary"),
        name="rglru",
    )(zb3, zb3, h0, hist0, cw, cb, wa4, ba, wx4, bx, lam)


def _s5_setup_kernel(are_ref, aim_ref, ldt_ref, bre_ref, bim_ref, cre_ref, cim_ref,
                     bd_ref, gm_ref, hm_ref, pre_ref, pim_ref):
    a_re = are_ref[...]
    a_im = aim_ref[...]
    dt = jnp.exp(ldt_ref[...])
    mag = jnp.exp(dt * a_re)
    ab_re = mag * jnp.cos(dt * a_im)
    ab_im = mag * jnp.sin(dt * a_im)
    den = a_re * a_re + a_im * a_im
    f_re = ((ab_re - 1.0) * a_re + ab_im * a_im) / den
    f_im = (ab_im * a_re - (ab_re - 1.0) * a_im) / den
    b_re = bre_ref[...]
    b_im = bim_ref[...]
    bb_re = f_re[:, None, :] * b_re - f_im[:, None, :] * b_im
    bb_im = f_re[:, None, :] * b_im + f_im[:, None, :] * b_re
    c_re = cre_ref[...]
    c_im = cim_ref[...]
    rhs = jnp.concatenate([bb_re, bb_im], axis=2)
    gb, n_c, n_p = c_re.shape
    gc = gb * n_c
    sw = gb * n_p
    ii = lambda shape, dim: lax.broadcasted_iota(jnp.int32, shape, dim)
    same = lambda shape, rdiv, ldiv: ii(shape, 0) // rdiv == ii(shape, 1) // ldiv
    tile_p = (ii((n_p, sw), 0) == ii((n_p, sw), 1) % n_p).astype(BF16)
    tile_c = (ii((n_c, gc), 0) == ii((n_c, gc), 1) % n_c).astype(BF16)
    wide = lambda x, tile: jnp.dot(x.astype(BF16), tile, preferred_element_type=F32)
    lane_grp = ii((n_p, gc), 1) // n_c
    p_re = jnp.ones_like(a_re)
    p_im = jnp.zeros_like(a_re)
    for tau in range(S5_CHUNK + 1):
        pre_ref[:, tau, :] = p_re
        pim_ref[:, tau, :] = p_im
        m_re = c_re * p_re[:, None, :] - c_im * p_im[:, None, :]
        m_im = c_re * p_im[:, None, :] + c_im * p_re[:, None, :]
        if tau < S5_CHUNK:
            s = S5_CHUNK - 1 - tau
            pb_re = p_re[:, None, :] * bb_re - p_im[:, None, :] * bb_im
            pb_im = p_re[:, None, :] * bb_im + p_im[:, None, :] * bb_re
            for ri, pb in enumerate((pb_re, pb_im)):
                blk = jnp.where(same((gc, sw), n_c, n_p), wide(pb.reshape(gc, n_p), tile_p), 0.0)
                gm_ref[0, s * gc:(s + 1) * gc, ri * sw:(ri + 1) * sw] = blk.astype(BF16)
            lhs = jnp.concatenate([m_re, -m_im], axis=2)
            k = jnp.einsum("gmk,gnk->gmn", lhs, rhs, precision=lax.Precision.HIGHEST,
                           preferred_element_type=F32)
            kw = jnp.where(same((gc, gc), n_c, n_c), wide(k.reshape(gc, n_c), tile_c), 0.0)
            bd_ref[0, tau] = kw.T.astype(BF16)
        if tau >= 1:
            for ri, m in enumerate((m_re, -m_im)):
                m2 = m.reshape(gc, n_p).astype(BF16).astype(F32)
                mt = jnp.concatenate([m2, jnp.zeros_like(m2)], axis=1).T[:n_p]
                for g in range(gb):
                    hm_ref[0, ri * sw + g * n_p:ri * sw + (g + 1) * n_p, (tau - 1) * gc:tau * gc] = (
                        jnp.where(lane_grp == g, mt, 0.0).astype(BF16))
        p_re, p_im = p_re * ab_re - p_im * ab_im, p_re * ab_im + p_im * ab_re


def _s5_setup(a_re, a_im, log_dt, b_re_t, b_im_t, c_re, c_im):
    g, p = a_re.shape
    c = c_re.shape[1]
    gb = LANES // c
    n_unit = g // gb
    nt = S5_CHUNK + 1
    lw = S5_CHUNK * LANES
    sw = gb * p
    g2 = pl.BlockSpec((gb, p), lambda i: (i, 0))
    g3 = pl.BlockSpec((gb, c, p), lambda i: (i, 0, 0))
    o3 = pl.BlockSpec((gb, nt, p), lambda i: (i, 0, 0))
    s3 = jax.ShapeDtypeStruct((g, nt, p), F32)
    return pl.pallas_call(
        _s5_setup_kernel,
        grid=(n_unit,),
        in_specs=[g2, g2, pl.BlockSpec((gb, 1), lambda i: (i, 0)), g3, g3, g3, g3],
        out_specs=[pl.BlockSpec((1, S5_CHUNK, LANES, LANES), lambda i: (i, 0, 0, 0)),
                   pl.BlockSpec((1, lw, 2 * sw), lambda i: (i, 0, 0)),
                   pl.BlockSpec((1, 2 * sw, lw), lambda i: (i, 0, 0)), o3, o3],
        out_shape=[jax.ShapeDtypeStruct((n_unit, S5_CHUNK, LANES, LANES), BF16),
                   jax.ShapeDtypeStruct((n_unit, lw, 2 * sw), BF16),
                   jax.ShapeDtypeStruct((n_unit, 2 * sw, lw), BF16), s3, s3],
        compiler_params=_params("parallel"),
        name="s5_setup",
    )(a_re, a_im, log_dt, b_re_t, b_im_t, c_re, c_im)


def _s5_kernel(u_ref, h0r_ref, h0i_ref, bd_ref, gm_ref, hm_ref, ar_ref, ai_ref,
               y_ref, her_ref, hei_ref, kt_scr, hs_scr, *, chunk, n_chunk):
    sw = ar_ref.shape[-1]
    rows = u_ref.shape[0] // chunk

    for s in range(chunk):
        for t in range(s, chunk):
            kt_scr[s * LANES:(s + 1) * LANES, t * LANES:(t + 1) * LANES] = bd_ref[0, t - s]
        if s % 2:
            kt_scr[s * LANES:(s + 1) * LANES, (s - 1) * LANES:s * LANES] = jnp.zeros((LANES, LANES), BF16)

    ucat = jnp.concatenate(
        [u_ref[pl.ds(s, rows, stride=chunk), :] for s in range(chunk)], axis=1).astype(BF16)
    gu = jnp.dot(ucat, gm_ref[0], preferred_element_type=F32)
    a_r = ar_ref[0]
    a_i = ai_ref[0]

    def advance(h_r, h_i, g):
        return a_r * h_r - a_i * h_i + g[:, :sw], a_r * h_i + a_i * h_r + g[:, sw:]

    if n_chunk == 1:
        h_r = h0r_ref[...]
        h_i = h0i_ref[...]
        hs = jnp.concatenate([h_r, h_i], axis=1)
        e_r, e_i = advance(h_r, h_i, gu)
        her_ref[...] = e_r
        hei_ref[...] = e_i
    else:
        hs_scr[...] = gu
        group = SUBLANES if n_chunk % SUBLANES == 0 else n_chunk
        sub = lax.broadcasted_iota(jnp.int32, (group, 2 * sw), 0)
        n_seq = rows // n_chunk

        def tile(it, hs_all):
            out = []
            for b in range(n_seq):
                h_r, h_i = hs_all[b]
                off = pl.multiple_of(b * n_chunk + it * group, group)
                g8 = hs_scr[pl.ds(off, group), :]
                hs8 = jnp.zeros((group, 2 * sw), F32)
                for i in range(group):
                    hs8 = jnp.where(sub == i, jnp.concatenate([h_r, h_i], axis=1), hs8)
                    h_r, h_i = advance(h_r, h_i, g8[i:i + 1, :])
                hs_scr[pl.ds(off, group), :] = hs8
                out.append((h_r, h_i))
            return tuple(out)

        ends = lax.fori_loop(0, n_chunk // group, tile,
                             tuple((h0r_ref[b], h0i_ref[b]) for b in range(n_seq)))
        for b in range(n_seq):
            her_ref[b] = ends[b][0]
            hei_ref[b] = ends[b][1]
        hs = hs_scr[...]
    hsb = hs.astype(BF16)
    for p in range(chunk // 2):
        k_hi = (2 * p + 2) * LANES
        cols = slice(2 * p * LANES, k_hi)
        y = (jnp.dot(ucat[:, :k_hi], kt_scr[:k_hi, cols], preferred_element_type=F32)
             + jnp.dot(hsb, hm_ref[0, :, cols], preferred_element_type=F32))
        for q in range(2):
            y_ref[pl.ds(2 * p + q, rows, stride=chunk), :] = y[:, q * LANES:(q + 1) * LANES]


def _s5_conv(u, h0r, h0i, bd, gm, hm, ar, ai, b, t, chunk):
    n, d = u.shape
    n_unit = d // LANES
    sw = ar.shape[-1]
    n_chunk = t // chunk
    lw = chunk * LANES
    if n_chunk == 1:
        hspec = pl.BlockSpec((b, sw), lambda q: (0, q))
        hshape = jax.ShapeDtypeStruct((b, n_unit * sw), F32)
    else:
        h0r, h0i = h0r[:, None, :], h0i[:, None, :]
        hspec = pl.BlockSpec((b, 1, sw), lambda q: (0, 0, q))
        hshape = jax.ShapeDtypeStruct((b, 1, n_unit * sw), F32)
    rows = b * n_chunk
    tok = pl.BlockSpec((n, LANES), lambda q: (0, q))
    y, he_r, he_i = pl.pallas_call(
        functools.partial(_s5_kernel, chunk=chunk, n_chunk=n_chunk),
        grid=(n_unit,),
        in_specs=[
            tok, hspec, hspec,
            pl.BlockSpec((1, chunk, LANES, LANES), lambda q: (q, 0, 0, 0)),
            pl.BlockSpec((1, lw, 2 * sw), lambda q: (q, S5_CHUNK // chunk - 1, 0)),
            pl.BlockSpec((1, 2 * sw, lw), lambda q: (q, 0, 0)),
            pl.BlockSpec((1, 1, sw), lambda q: (q, 0, 0)),
            pl.BlockSpec((1, 1, sw), lambda q: (q, 0, 0)),
        ],
        out_specs=[tok, hspec, hspec],
        out_shape=[jax.ShapeDtypeStruct((n, d), F32), hshape, hshape],
        scratch_shapes=[pltpu.VMEM((lw, lw), BF16), pltpu.VMEM((rows, 2 * sw), F32)],
        compiler_params=_params("parallel"),
        name="s5_conv",
    )(u, h0r, h0i, bd, gm, hm, ar, ai)
    return y, he_r.reshape(b, -1), he_i.reshape(b, -1)


def _s5_out_kernel(x_ref, u_ref, yc_ref, d_ref, w_ref, b_ref, o_ref):
    z = jax.nn.gelu(yc_ref[...] + d_ref[...] * u_ref[...])
    o_ref[...] = x_ref[...] + z * jax.nn.sigmoid(_bdot(z, w_ref[...]) + b_ref[...])


def _s5_out(x, u, yc, d, w, b):
    n, dm = x.shape
    tm = _row_tile(n, 256)
    row = lambda i: (0, 0)
    tile = pl.BlockSpec((tm, dm), lambda i: (i, 0))
    vec = pl.BlockSpec((1, dm), row)
    return pl.pallas_call(
        _s5_out_kernel,
        grid=(n // tm,),
        in_specs=[tile, tile, tile, vec, pl.BlockSpec((dm, dm), row), vec],
        out_specs=tile,
        out_shape=jax.ShapeDtypeStruct((n, dm), F32),
        compiler_params=_params("parallel"),
        name="s5_out",
    )(x, u, yc, d, w, b)


def _block_diag_tiles(w, per_tile):
    n_blk, h, _ = w.shape
    n_tile = n_blk // per_tile
    w = w.reshape(n_tile, per_tile, h, h)
    eye = jnp.eye(per_tile, dtype=w.dtype)
    out = jnp.einsum("tphk,pq->tphqk", w, eye)
    return out.reshape(n_tile, per_tile * h, per_tile * h)


def _head_sum_mats(wa):
    onehot = (jnp.arange(wa)[:, None] // HEAD == jnp.arange(LANES)[None, :])
    return onehot.astype(BF16), onehot.T.astype(BF16)


def _wkv_on_lanes(vecs, v, s0):
    b, t, wa = v.shape
    h = wa // HEAD
    t8 = t // SUBLANES
    tt = min(t, 64)
    if b * h <= LANES:
        rep = LANES // (b * h)
        ip = HEAD // rep
        tp = min(t, LANES)
        keys = [_to_lanes(x, rep, False, tp)[None] for x in vecs]
        v2 = _to_lanes(v, rep, True, tp).reshape(1, t, ip, LANES)
        s2 = s0.reshape(b, h, rep, ip, HEAD).transpose(4, 3, 2, 0, 1).reshape(1, HEAD, ip, LANES)
        y2, s2 = _wkv(keys, v2, s2, tt)
        y = _from_lanes(y2.reshape(t * ip, LANES), b, wa, rep, tp)
        s = s2.reshape(HEAD, ip, rep, b, h).transpose(3, 4, 2, 1, 0).reshape(b, h, HEAD, HEAD)
    else:
        nb = b // LANES
        lay = lambda x: x.reshape(nb, LANES, t8, SUBLANES, h, HEAD).transpose(4, 0, 5, 2, 3, 1).reshape(
            h * nb, HEAD, t8, SUBLANES, LANES)
        v2 = v.reshape(nb, LANES, t, h, HEAD).transpose(3, 0, 2, 4, 1).reshape(h * nb, t, HEAD, LANES)
        s2 = s0.reshape(nb, LANES, h, HEAD, HEAD).transpose(2, 0, 4, 3, 1).reshape(h * nb, HEAD, HEAD, LANES)
        y2, s2 = _wkv([lay(x) for x in vecs], v2, s2, tt)
        y = y2.reshape(h, nb, t, HEAD, LANES).transpose(1, 4, 2, 0, 3).reshape(b, t, wa)
        s = s2.reshape(h, nb, HEAD, HEAD, LANES).transpose(1, 4, 0, 3, 2).reshape(b, h, HEAD, HEAD)
    return y, s


def _seq_blocks(b, t):
    if t >= 256:
        return 1, 256
    return min(b, max(1, 256 // t)), t


def _prepare(w):
    depth = w["norm_ffn1"].shape[0]
    wa = w["w0_a"].shape[1]
    wb = w["lam_b"].shape[1]
    cols_a = w["mu_a"].shape[1]
    ca = ((cols_a + 511) // 512) * 512
    row = lambda v: v.reshape(1, -1)
    bf = lambda v: v.astype(BF16)
    pad_a = lambda v: jnp.pad(v, [(0, 0)] * (v.ndim - 1) + [(0, ca - cols_a)])
    layers = []
    for l in range(depth):
        j = l // 2
        q = dict(ple_gate=bf(w["ple_gate"][l]), ple_proj=bf(w["ple_proj"][l]))
        if l % 2 == 0:
            w_in = w["w_in_ab"][j]
            n_q = wb // (2 * LANES)
            w_out = bf(w["w_out_ab"][j])
            q.update(
                w_in_a=bf(pad_a(w_in[:, :cols_a])), w_in_b=bf(w_in[:, cols_a:]), mu=row(pad_a(w["mu_a"][j])),
                w2=bf(jnp.zeros((LANES, wa), F32).at[:LORA_W].set(w["w2_a"][j])),
                a2=bf(jnp.zeros((LANES, wa), F32).at[LORA_W:LORA_W + LORA_A].set(w["a2_a"][j])),
                g2=bf(jnp.zeros((2 * LANES, wa), F32).at[:LORA_G].set(w["g2_a"][j])),
                wa4=bf(_block_diag_tiles(w["wa_b"][j], (wb // HEAD) // n_q)),
                wx4=bf(_block_diag_tiles(w["wx_b"][j], (wb // HEAD) // n_q)),
                w_out_a=w_out[:wa], w_out_b=w_out[wa:])
        else:
            q.update(
                tabs=_s5_setup(w["a_re_c"][j], w["a_im_c"][j], w["log_dt_c"][j][:, None],
                               w["b_re_c"][j].transpose(0, 2, 1), w["b_im_c"][j].transpose(0, 2, 1),
                               w["c_re_c"][j], w["c_im_c"][j]),
                w_glu=bf(w["w_glu_c"][j]))
        layers.append(q)
    return layers


def _trunk(x, p, st_wkv, st_shift, st_h, st_conv, st_cre, st_cim, w, wp):
    b, t, d = x.shape
    n = b * t
    depth = w["norm_ffn1"].shape[0]
    wa = w["w0_a"].shape[1]
    wb = w["lam_b"].shape[1]
    cols_a = w["mu_a"].shape[1]
    ca = wp[0]["mu"].shape[1]
    n_head = wa // HEAD
    n_grp = w["a_re_c"].shape[1]
    chunk = min(S5_CHUNK, t)
    ered, eexp = _head_sum_mats(wa)
    row = lambda v: v.reshape(1, -1)

    x = x.reshape(n, d)
    new = {k: [] for k in ("wkv", "shift", "h", "conv", "cre", "cim")}
    for l in range(depth):
        j = l // 2
        q = wp[l]
        x = _ffn(x, row(w["norm_ffn1"][l]), w["ffn1_wg"], w["ffn1_wu"], w["ffn1_wd"], l)
        if l % 2 == 0:
            za, zb = _inproj(x, row(w["norm_mix"][l]), q["w_in_a"], q["w_in_b"])
            z3 = za.reshape(b, t, ca)
            zb3 = zb.reshape(b, t, 2 * wb)
            shift = jnp.pad(st_shift[j], ((0, 0), (0, ca - cols_a)))[:, None, :]
            bb, tt = _seq_blocks(b, t)
            r, dec, k2, v, nkk, kka, g = _rwkv_prep(
                z3, shift, q["mu"], row(w["w0_a"][j]), q["w2"], row(w["a0_a"][j]), q["a2"], q["g2"],
                row(w["kk_a"][j]), row(w["ka_a"][j]), ered, eexp, wa, ca, bb, tt)
            y, s_new = _wkv_on_lanes([nkk, dec, kka, k2, r], v, st_wkv[j])
            hist0 = jnp.pad(st_conv[j], ((0, 0), (SUBLANES - (CONV_W - 1), 0), (0, 0)))
            yb, h_new = _rglru(
                zb3, st_h[j][:, None, :], hist0, w["conv_w_b"][j], row(w["conv_b_b"][j]), q["wa4"],
                row(w["ba_b"][j]), q["wx4"], row(w["bx_b"][j]), row(w["lam_b"][j]), bb, tt)
            f2 = lambda a: a.reshape(n, -1)
            x = _mix_out(x, f2(y), f2(r), f2(k2), f2(v), f2(g), f2(yb), row(w["rk_a"][j]),
                         row(w["lnx_g"][j]), row(w["lnx_b"][j]), ered, eexp, q["w_out_a"], q["w_out_b"])
            new["wkv"].append(s_new)
            new["shift"].append(z3[:, t - 1, :cols_a])
            new["h"].append(h_new[:, 0, :])
            conv_all = jnp.concatenate([st_conv[j], zb3[:, max(t - (CONV_W - 1), 0):, :wb]], axis=1)
            new["conv"].append(conv_all[:, conv_all.shape[1] - (CONV_W - 1):])
        else:
            u = _rmsnorm(x, row(w["norm_mix"][l]))
            bd, gm, hm, p_re, p_im = q["tabs"]
            ar = p_re[:, chunk].reshape(bd.shape[0], 1, -1)
            ai = p_im[:, chunk].reshape(bd.shape[0], 1, -1)
            yc, he_r, he_i = _s5_conv(u, st_cre[j].reshape(b, -1), st_cim[j].reshape(b, -1),
                                      bd, gm, hm, ar, ai, b, t, chunk)
            x = _s5_out(x, u, yc, row(w["d_c"][j]), q["w_glu"], row(w["b_glu_c"][j]))
            new["cre"].append(he_r.reshape(b, n_grp, P_C))
            new["cim"].append(he_i.reshape(b, n_grp, P_C))
        x = _ffn(x, row(w["norm_ffn2"][l]), w["ffn2_wg"], w["ffn2_wu"], w["ffn2_wd"], l)
        x = _ple(x, row(w["norm_ple"][l]), q["ple_gate"], p[l].reshape(n, -1), q["ple_proj"],
                 row(w["final_norm"]) if l == depth - 1 else None)
    y = x.reshape(b, t, d)
    stk = lambda name, ref: jnp.stack(new[name]).astype(ref.dtype)
    return y, (stk("wkv", st_wkv), stk("shift", st_shift), stk("h", st_h),
               stk("conv", st_conv), stk("cre", st_cre), stk("cim", st_cim))


def kernel(x_prompt, x_sample, state_a_wkv, state_a_shift, state_b_h, state_b_conv, state_c_re, state_c_im, p_prompt, p_sample, norm_ffn1, ffn1_wg, ffn1_wu, ffn1_wd, norm_mix, norm_ffn2, ffn2_wg, ffn2_wu, ffn2_wd, norm_ple, ple_gate, ple_proj, w_in_ab, mu_a, w0_a, w2_a, a0_a, a2_a, g2_a, kk_a, ka_a, rk_a, lnx_g, lnx_b, conv_w_b, conv_b_b, wa_b, ba_b, wx_b, bx_b, lam_b, w_out_ab, a_re_c, a_im_c, log_dt_c, b_re_c, b_im_c, c_re_c, c_im_c, d_c, w_glu_c, b_glu_c, final_norm):
    w = dict(norm_ffn1=norm_ffn1, ffn1_wg=ffn1_wg, ffn1_wu=ffn1_wu, ffn1_wd=ffn1_wd,
             norm_mix=norm_mix, norm_ffn2=norm_ffn2, ffn2_wg=ffn2_wg, ffn2_wu=ffn2_wu,
             ffn2_wd=ffn2_wd, norm_ple=norm_ple, ple_gate=ple_gate, ple_proj=ple_proj,
             w_in_ab=w_in_ab, mu_a=mu_a, w0_a=w0_a, w2_a=w2_a, a0_a=a0_a, a2_a=a2_a, g2_a=g2_a,
             kk_a=kk_a, ka_a=ka_a, rk_a=rk_a.reshape(rk_a.shape[0], -1), lnx_g=lnx_g, lnx_b=lnx_b,
             conv_w_b=conv_w_b, conv_b_b=conv_b_b, wa_b=wa_b, ba_b=ba_b, wx_b=wx_b, bx_b=bx_b,
             lam_b=lam_b, w_out_ab=w_out_ab,
             a_re_c=a_re_c, a_im_c=a_im_c, log_dt_c=log_dt_c, b_re_c=b_re_c, b_im_c=b_im_c,
             c_re_c=c_re_c, c_im_c=c_im_c, d_c=d_c, w_glu_c=w_glu_c, b_glu_c=b_glu_c,
             final_norm=final_norm)
    wp = _prepare(w)
    bp = x_prompt.shape[0]
    zeros = lambda s: jnp.zeros((s.shape[0], bp) + s.shape[2:], s.dtype)
    y_prompt, prompt_state = _trunk(
        x_prompt, p_prompt, zeros(state_a_wkv), zeros(state_a_shift), zeros(state_b_h),
        zeros(state_b_conv), zeros(state_c_re), zeros(state_c_im), w, wp)
    y_sample, sample_state = _trunk(
        x_sample, p_sample, state_a_wkv, state_a_shift, state_b_h, state_b_conv,
        state_c_re, state_c_im, w, wp)
    return (y_prompt, y_sample) + tuple(prompt_state) + tuple(sample_state)
```

```python
import functools

import jax
import jax.numpy as jnp
from jax import lax
from jax.experimental import pallas as pl
from jax.experimental.pallas import tpu as pltpu

F32 = jnp.float32
BF16 = jnp.bfloat16

HEAD = 64
LANES = 128
SUBLANES = 8
LORA_W = 64
LORA_A = 64
LORA_G = 160
CONV_W = 4
LRU_C = 8.0
GRP_C = 16
P_C = 64
RMS_EPS = 1e-6
GN_EPS = 64e-5
S5_CHUNK = 16
N_ACC = 4
VMEM_LIMIT = 56 * 1024 * 1024


def _params(*sem):
    return pltpu.CompilerParams(dimension_semantics=sem, vmem_limit_bytes=VMEM_LIMIT)


def _rms(x, g):
    return x * lax.rsqrt(jnp.mean(x * x, axis=-1, keepdims=True) + RMS_EPS) * g


def _softplus(x):
    return jnp.maximum(x, 0.0) + jnp.log1p(jnp.exp(-jnp.abs(x)))


def _bdot(a, b):
    return jnp.dot(a.astype(BF16), b, preferred_element_type=F32)


def _split_dot(x, e):
    hi = x.astype(BF16)
    lo = (x - hi.astype(F32)).astype(BF16)
    return (jnp.dot(hi, e, preferred_element_type=F32)
            + jnp.dot(lo, e, preferred_element_type=F32))


def _head_sum(x, e_red, e_exp):
    return _split_dot(_split_dot(x, e_red), e_exp)


def _row_tile(n, pref):
    t = min(n, pref)
    while n % t:
        t //= 2
    return t


def _ffn_kernel(x_hbm, g_ref, wg_hbm, wu_hbm, wd_hbm, o_ref, h_ref, x_buf, wg_buf, wu_buf, wd_buf, x_sem, w_sem,
                *, layer):
    i = pl.program_id(0)
    last_row = i + 1 == pl.num_programs(0)
    tm = x_buf.shape[0]
    tf = wg_buf.shape[2]
    n_f = wg_hbm.shape[2] // tf

    def x_copy(tile):
        return pltpu.make_async_copy(x_hbm.at[pl.ds(tile * tm, tm)], x_buf, x_sem)

    def w_copies(j, slot):
        cols = pl.ds(pl.multiple_of(j * tf, tf), tf)
        return (pltpu.make_async_copy(wg_hbm.at[layer, :, cols], wg_buf.at[slot], w_sem.at[0, slot]),
                pltpu.make_async_copy(wu_hbm.at[layer, :, cols], wu_buf.at[slot], w_sem.at[1, slot]),
                pltpu.make_async_copy(wd_hbm.at[layer, cols, :], wd_buf.at[slot], w_sem.at[2, slot]))

    @pl.when(i == 0)
    def _():
        x_copy(0).start()
        for c in w_copies(0, 0):
            c.start()

    x_copy(i).wait()
    x = x_buf[...]
    h_ref[...] = _rms(x, g_ref[...]).astype(BF16)
    o_ref[...] = x

    @pl.when(jnp.logical_not(last_row))
    def _():
        x_copy(i + 1).start()

    def tile(j, carry):
        slot = j % 2
        for c in w_copies(j, slot):
            c.wait()

        @pl.when(jnp.logical_not(jnp.logical_and(last_row, j + 1 == n_f)))
        def _():
            for c in w_copies((j + 1) % n_f, 1 - slot):
                c.start()

        h = h_ref[...]
        a = jnp.dot(h, wg_buf[slot].astype(BF16), preferred_element_type=F32)
        b = jnp.dot(h, wu_buf[slot].astype(BF16), preferred_element_type=F32)
        hid = (0.5 * a * jax.nn.sigmoid(a)) * b
        o_ref[...] += jnp.dot(hid.astype(BF16), wd_buf[slot].astype(BF16), preferred_element_type=F32)
        return carry

    lax.fori_loop(0, n_f, tile, 0)


def _ffn(x, g, wg, wu, wd, l):
    n, d = x.shape
    f = wg.shape[2]
    tm = _row_tile(n, 1024)
    tf = _row_tile(f, 256)
    assert (f // tf) % 2 == 0, "the two weight slots must line up across row tiles"
    hbm = pl.BlockSpec(memory_space=pl.ANY)
    return pl.pallas_call(
        functools.partial(_ffn_kernel, layer=l),
        grid=(n // tm,),
        in_specs=[hbm, pl.BlockSpec((1, d), lambda i: (0, 0)), hbm, hbm, hbm],
        out_specs=pl.BlockSpec((tm, d), lambda i: (i, 0)),
        out_shape=jax.ShapeDtypeStruct((n, d), F32),
        scratch_shapes=[pltpu.VMEM((tm, d), BF16), pltpu.VMEM((tm, d), F32),
                        pltpu.VMEM((2, d, tf), F32), pltpu.VMEM((2, d, tf), F32), pltpu.VMEM((2, tf, d), F32),
                        pltpu.SemaphoreType.DMA(()), pltpu.SemaphoreType.DMA((3, 2))],
        compiler_params=_params("arbitrary"),
        name="ffn",
    )(x, g, wg, wu, wd)


def _ple_kernel(x_ref, g_ref, wgate_ref, p_ref, wproj_ref, *rest):
    o_ref = rest[-1]
    x = x_ref[...]
    gate = jax.nn.sigmoid(_bdot(_rms(x, g_ref[...]), wgate_ref[...]))
    y = x + gate * _bdot(p_ref[...], wproj_ref[...])
    o_ref[...] = _rms(y, rest[0][...]) if len(rest) == 2 else y


def _ple(x, g, wgate, p, wproj, final_g=None):
    n, d = x.shape
    dp = p.shape[1]
    tm = _row_tile(n, 512)
    vec = pl.BlockSpec((1, d), lambda i: (0, 0))
    extra = [] if final_g is None else [final_g]
    return pl.pallas_call(
        _ple_kernel,
        grid=(n // tm,),
        in_specs=[
            pl.BlockSpec((tm, d), lambda i: (i, 0)),
            vec,
            pl.BlockSpec((d, d), lambda i: (0, 0)),
            pl.BlockSpec((tm, dp), lambda i: (i, 0)),
            pl.BlockSpec((dp, d), lambda i: (0, 0)),
        ] + [vec] * len(extra),
        out_specs=pl.BlockSpec((tm, d), lambda i: (i, 0)),
        out_shape=jax.ShapeDtypeStruct((n, d), F32),
        compiler_params=_params("parallel"),
        name="ple",
    )(x, g, wgate, p, wproj, *extra)


def _rmsnorm_kernel(x_ref, g_ref, o_ref):
    o_ref[...] = _rms(x_ref[...], g_ref[...])


def _rmsnorm(x, g):
    n, d = x.shape
    tm = _row_tile(n, 512)
    return pl.pallas_call(
        _rmsnorm_kernel,
        grid=(n // tm,),
        in_specs=[pl.BlockSpec((tm, d), lambda i: (i, 0)),
                  pl.BlockSpec((1, d), lambda i: (0, 0))],
        out_specs=pl.BlockSpec((tm, d), lambda i: (i, 0)),
        out_shape=jax.ShapeDtypeStruct((n, d), F32),
        compiler_params=_params("parallel"),
        name="rmsnorm",
    )(x, g)


def _inproj_kernel(x_ref, g_ref, wa_ref, wb_ref, za_ref, zb_ref):
    h = _rms(x_ref[...], g_ref[...]).astype(BF16)
    za_ref[...] = jnp.dot(h, wa_ref[...], preferred_element_type=F32)
    zb_ref[...] = jnp.dot(h, wb_ref[...], preferred_element_type=F32)


def _inproj(x, g, w_a, w_b):
    n, d = x.shape
    ca, cb = w_a.shape[1], w_b.shape[1]
    tm = _row_tile(n, 256)
    fixed = lambda shape: pl.BlockSpec(shape, lambda i: (0, 0), pipeline_mode=pl.Buffered(1))
    return pl.pallas_call(
        _inproj_kernel,
        grid=(n // tm,),
        in_specs=[pl.BlockSpec((tm, d), lambda i: (i, 0)), pl.BlockSpec((1, d), lambda i: (0, 0)),
                  fixed((d, ca)), fixed((d, cb))],
        out_specs=[pl.BlockSpec((tm, ca), lambda i: (i, 0)), pl.BlockSpec((tm, cb), lambda i: (i, 0))],
        out_shape=[jax.ShapeDtypeStruct((n, ca), F32), jax.ShapeDtypeStruct((n, cb), F32)],
        compiler_params=_params("parallel"),
        name="inproj",
    )(x, g, w_a, w_b)


def _rwkv_prep_kernel(z_ref, shift_ref, mu_ref, w0_ref, w2_ref, a0_ref, a2_ref, g2_ref,
                      kk_ref, ka_ref, ered_ref, eexp_ref,
                      r_ref, w_ref, k_ref, v_ref, nkk_ref, kka_ref, g_ref, carry_ref, *, wa):
    tb = pl.program_id(1)
    z = z_ref[...]
    bb, tt, ca = z.shape

    @pl.when(tb == 0)
    def _():
        carry_ref[...] = shift_ref[...]

    t_idx = lax.broadcasted_iota(jnp.int32, z.shape, 1)
    zprev = jnp.where(t_idx == 0, carry_ref[...], pltpu.roll(z, 1, axis=1))
    carry_ref[...] = z[:, tt - 1:tt, :]
    zs = (z + (zprev - z) * mu_ref[...]).reshape(bb * tt, ca)

    r = zs[:, 0:wa]
    k = zs[:, wa:2 * wa]
    v = zs[:, 2 * wa:3 * wa]
    xwa = zs[:, 3 * wa:3 * wa + LANES]
    xg = zs[:, 3 * wa + LANES:3 * wa + 3 * LANES]
    w_log = -_softplus(-(w0_ref[...] + _bdot(jnp.tanh(xwa), w2_ref[...]))) - 0.5
    decay = jnp.exp(-jnp.exp(w_log))
    a = jax.nn.sigmoid(a0_ref[...] + _bdot(xwa, a2_ref[...]))
    g = _bdot(jax.nn.sigmoid(xg), g2_ref[...])
    kk = k * kk_ref[...]
    kn = kk / jnp.maximum(jnp.sqrt(_head_sum(kk * kk, ered_ref[...], eexp_ref[...])), 1e-12)
    k2 = k * (1.0 + (a - 1.0) * ka_ref[...])
    shp = (bb, tt, wa)
    r_ref[...] = r.reshape(shp)
    w_ref[...] = decay.reshape(shp)
    k_ref[...] = k2.reshape(shp)
    v_ref[...] = v.reshape(shp)
    nkk_ref[...] = (-kn).reshape(shp)
    kka_ref[...] = (kn * a).reshape(shp)
    g_ref[...] = g.reshape(shp)


def _rwkv_prep(z3, shift, mu, w0, w2p, a0, a2p, g2p, kk, ka, ered, eexp, wa, ca, bb, tt):
    b, t, _ = z3.shape
    row = lambda i, j: (0, 0)
    vec = lambda c: pl.BlockSpec((1, c), row)
    out_spec = pl.BlockSpec((bb, tt, wa), lambda i, j: (i, j, 0))
    out_shape = jax.ShapeDtypeStruct((b, t, wa), F32)
    return pl.pallas_call(
        functools.partial(_rwkv_prep_kernel, wa=wa),
        grid=(b // bb, t // tt),
        in_specs=[
            pl.BlockSpec((bb, tt, ca), lambda i, j: (i, j, 0)),
            pl.BlockSpec((bb, 1, ca), lambda i, j: (i, 0, 0)),
            vec(ca), vec(wa),
            pl.BlockSpec(w2p.shape, row), vec(wa),
            pl.BlockSpec(a2p.shape, row), pl.BlockSpec(g2p.shape, row),
            vec(wa), vec(wa),
            pl.BlockSpec(ered.shape, row), pl.BlockSpec(eexp.shape, row),
        ],
        out_specs=[out_spec] * 7,
        out_shape=[out_shape] * 7,
        scratch_shapes=[pltpu.VMEM((bb, 1, ca), F32)],
        compiler_params=_params("parallel", "arbitrary"),
        name="rwkv_prep",
    )(z3, shift, mu, w0, w2p, a0, a2p, g2p, kk, ka, ered, eexp)


def _to_lanes_kernel(x_ref, o_ref, xt_scr, *, rep, split):
    bsz, tt, wa = x_ref.shape
    n_head = wa // HEAD
    n_rows = HEAD // rep if split else HEAD
    for b in range(bsz):
        xt_scr[b] = x_ref[b].T
    for r in range(n_rows):
        offs = [r + q * n_rows for q in range(rep)] if split else [r] * rep
        m = jnp.concatenate(
            [xt_scr[b, pl.ds(off, n_head, stride=HEAD), :] for off in offs for b in range(bsz)], axis=0).T
        if split:
            o_ref[pl.ds(r, tt, stride=n_rows), :] = m
        else:
            o_ref[r] = m.reshape(tt // SUBLANES, SUBLANES, LANES)


def _to_lanes(x, rep, split, tt):
    b, t, wa = x.shape
    n_rows = HEAD // rep if split else HEAD
    if split:
        out_spec = pl.BlockSpec((tt * n_rows, LANES), lambda i: (i, 0))
        out_shape = jax.ShapeDtypeStruct((t * n_rows, LANES), F32)
    else:
        out_spec = pl.BlockSpec((HEAD, tt // SUBLANES, SUBLANES, LANES), lambda i: (0, i, 0, 0))
        out_shape = jax.ShapeDtypeStruct((HEAD, t // SUBLANES, SUBLANES, LANES), F32)
    return pl.pallas_call(
        functools.partial(_to_lanes_kernel, rep=rep, split=split),
        grid=(t // tt,),
        in_specs=[pl.BlockSpec((b, tt, wa), lambda i: (0, i, 0))],
        out_specs=out_spec,
        out_shape=out_shape,
        scratch_shapes=[pltpu.VMEM((b, wa, tt), F32)],
        compiler_params=_params("parallel"),
        name="to_lanes",
    )(x)


def _from_lanes_kernel(y_ref, o_ref, yt_scr, *, rep):
    bsz, tt, wa = o_ref.shape
    n_head = wa // HEAD
    n_rows = HEAD // rep
    for r in range(n_rows):
        m = y_ref[pl.ds(r, tt, stride=n_rows), :].T
        for q in range(rep):
            for b in range(bsz):
                k0 = (q * bsz + b) * n_head
                yt_scr[b, pl.ds(q * n_rows + r, n_head, stride=HEAD), :] = m[k0:k0 + n_head, :]
    for b in range(bsz):
        o_ref[b] = yt_scr[b].T


def _from_lanes(y2, b, wa, rep, tt):
    n_rows = HEAD // rep
    t = y2.shape[0] // n_rows
    return pl.pallas_call(
        functools.partial(_from_lanes_kernel, rep=rep),
        grid=(t // tt,),
        in_specs=[pl.BlockSpec((tt * n_rows, LANES), lambda i: (i, 0))],
        out_specs=pl.BlockSpec((b, tt, wa), lambda i: (0, i, 0)),
        out_shape=jax.ShapeDtypeStruct((b, t, wa), F32),
        scratch_shapes=[pltpu.VMEM((b, wa, tt), F32)],
        compiler_params=_params("parallel"),
        name="from_lanes",
    )(y2)


def _wkv_kernel(nkk_ref, w_ref, kka_ref, k_ref, r_ref, v_ref, s0_ref, y_ref, s_ref):
    n_key, n_t8 = nkk_ref.shape[1], nkk_ref.shape[2]
    slab = s_ref.shape[2:]

    @pl.when(pl.program_id(1) == 0)
    def _():
        s_ref[...] = s0_ref[...]

    def tree_sum(parts):
        while len(parts) > 1:
            parts = [parts[i] + parts[i + 1] for i in range(0, len(parts), 2)]
        return parts[0]

    def tile_step(t8, carry):
        for i in range(SUBLANES):
            t = t8 * SUBLANES + i
            key_row = lambda ref, j: ref[0, j, t8, i:i + 1, :]
            acc = [jnp.zeros(slab, F32) for _ in range(N_ACC)]
            for j in range(n_key):
                acc[j % N_ACC] = acc[j % N_ACC] + s_ref[0, j] * key_row(nkk_ref, j)
            sa = tree_sum(acc)
            v = v_ref[0, t]
            acc = [jnp.zeros(slab, F32) for _ in range(N_ACC)]
            for j in range(n_key):
                s = s_ref[0, j] * key_row(w_ref, j) + sa * key_row(kka_ref, j) + v * key_row(k_ref, j)
                s_ref[0, j] = s
                acc[j % N_ACC] = acc[j % N_ACC] + s * key_row(r_ref, j)
            y_ref[0, t] = tree_sum(acc)
        return carry

    lax.fori_loop(0, n_t8, tile_step, 0)


def _wkv(keys, v, s0, tt):
    u, t, rows, _ = v.shape
    n_key = s0.shape[1]
    st = pl.BlockSpec((1, n_key, rows, LANES), lambda i, j: (i, 0, 0, 0))
    seq = pl.BlockSpec((1, tt, rows, LANES), lambda i, j: (i, j, 0, 0))
    key = pl.BlockSpec((1, n_key, tt // SUBLANES, SUBLANES, LANES), lambda i, j: (i, 0, j, 0, 0))
    return pl.pallas_call(
        _wkv_kernel,
        grid=(u, t // tt),
        in_specs=[key] * 5 + [seq, st],
        out_specs=[seq, st],
        out_shape=[jax.ShapeDtypeStruct(v.shape, F32), jax.ShapeDtypeStruct(s0.shape, F32)],
        compiler_params=_params("parallel", "arbitrary"),
        name="wkv",
    )(*keys, v, s0)


def _mix_out_kernel(x_ref, y_ref, r_ref, k_ref, v_ref, g_ref, yb_ref, rk_ref, lng_ref, lnb_ref,
                    ered_ref, eexp_ref, woa_ref, wob_ref, o_ref):
    ered = ered_ref[...]
    eexp = eexp_ref[...]
    y = y_ref[...]
    v = v_ref[...]
    mean = _head_sum(y, ered, eexp) * (1.0 / HEAD)
    d = y - mean
    var = _head_sum(d * d, ered, eexp) * (1.0 / HEAD)
    yn = d * lax.rsqrt(var + GN_EPS) * lng_ref[...] + lnb_ref[...]
    bonus = _head_sum(r_ref[...] * k_ref[...] * rk_ref[...], ered, eexp) * v
    ya = (yn + bonus) * g_ref[...]
    o_ref[...] = x_ref[...] + _bdot(ya, woa_ref[...]) + _bdot(yb_ref[...], wob_ref[...])


def _mix_out(x, y, r, k, v, g, yb, rk, lng, lnb, ered, eexp, woa, wob):
    n, d = x.shape
    wa = y.shape[1]
    wb = yb.shape[1]
    tm = _row_tile(n, 256)
    row = lambda i: (0, 0)
    ta = pl.BlockSpec((tm, wa), lambda i: (i, 0))
    va = pl.BlockSpec((1, wa), row)
    return pl.pallas_call(
        _mix_out_kernel,
        grid=(n // tm,),
        in_specs=[pl.BlockSpec((tm, d), lambda i: (i, 0)), ta, ta, ta, ta, ta,
                  pl.BlockSpec((tm, wb), lambda i: (i, 0)), va, va, va,
                  pl.BlockSpec(ered.shape, row), pl.BlockSpec(eexp.shape, row),
                  pl.BlockSpec(woa.shape, row), pl.BlockSpec(wob.shape, row)],
        out_specs=pl.BlockSpec((tm, d), lambda i: (i, 0)),
        out_shape=jax.ShapeDtypeStruct((n, d), F32),
        compiler_params=_params("parallel"),
        name="mix_out",
    )(x, y, r, k, v, g, yb, rk, lng, lnb, ered, eexp, woa, wob)


def _rglru_kernel(xb_ref, gb_ref, h0_ref, hist0_ref, cw_ref, cb_ref, wa_ref, ba_ref, wx_ref, bx_ref,
                  lam_ref, y_ref, hl_ref, hist_ref, h_ref, a_scr, b_scr):
    tb = pl.program_id(1)
    x = xb_ref[...]
    bb, tt, wb = x.shape
    n_tile = tt // SUBLANES

    @pl.when(tb == 0)
    def _():
        hist_ref[...] = hist0_ref[...]
        h_ref[...] = h0_ref[...]

    hist = hist_ref[...]
    hist_ref[...] = x[:, tt - SUBLANES:tt, :]
    cw = cw_ref[...]
    t8 = lax.broadcasted_iota(jnp.int32, (bb, SUBLANES, wb), 1)
    xc = cb_ref[...] + x * cw[CONV_W - 1:CONV_W, :]
    for dly in range(1, CONV_W):
        rolled = pltpu.roll(x, dly, axis=1)
        head = jnp.where(t8 < dly, pltpu.roll(hist, dly, axis=1), rolled[:, :SUBLANES, :])
        if n_tile > 1:
            shifted = jnp.concatenate([head, rolled[:, SUBLANES:, :]], axis=1)
        else:
            shifted = head
        xc = xc + shifted * cw[CONV_W - 1 - dly:CONV_W - dly, :]

    xc2 = xc.reshape(bb * tt, wb)
    xcb = xc2.astype(BF16)
    n_q = wa_ref.shape[0]
    wq = wb // n_q
    gr = jnp.concatenate(
        [jnp.dot(xcb[:, q * wq:(q + 1) * wq], wa_ref[q], preferred_element_type=F32) for q in range(n_q)],
        axis=1)
    gi = jnp.concatenate(
        [jnp.dot(xcb[:, q * wq:(q + 1) * wq], wx_ref[q], preferred_element_type=F32) for q in range(n_q)],
        axis=1)
    gate_r = jax.nn.sigmoid(gr + ba_ref[...])
    gate_i = jax.nn.sigmoid(gi + bx_ref[...])
    log_a = (-LRU_C) * gate_r * _softplus(-lam_ref[...])
    a = jnp.exp(log_a)
    b = jnp.sqrt(1.0 - jnp.exp(2.0 * log_a)) * (gate_i * xc2)
    a_scr[...] = a.reshape(bb, tt, wb)
    b_scr[...] = b.reshape(bb, tt, wb)

    def tile_scan(i, h):
        off = pl.multiple_of(i * SUBLANES, SUBLANES)
        at = a_scr[:, pl.ds(off, SUBLANES), :]
        bt = b_scr[:, pl.ds(off, SUBLANES), :]
        for dly in (1, 2, 4):
            keep = t8 >= dly
            bt = jnp.where(keep, at * pltpu.roll(bt, dly, axis=1) + bt, bt)
            at = jnp.where(keep, at * pltpu.roll(at, dly, axis=1), at)
        ht = bt + at * h
        b_scr[:, pl.ds(off, SUBLANES), :] = ht
        return ht[:, SUBLANES - 1:SUBLANES, :]

    h_last = lax.fori_loop(0, n_tile, tile_scan, h_ref[...])
    h_ref[...] = h_last
    hl_ref[...] = h_last
    y_ref[...] = b_scr[...] * jax.nn.gelu(gb_ref[...])


def _rglru(zb3, h0, hist0, cw, cb, wa4, ba, wx4, bx, lam, bb, tt):
    b, t, _ = zb3.shape
    wb = h0.shape[-1]
    row = lambda i, j: (0, 0)
    vec = pl.BlockSpec((1, wb), row)
    st = pl.BlockSpec((bb, 1, wb), lambda i, j: (i, 0, 0))
    return pl.pallas_call(
        _rglru_kernel,
        grid=(b // bb, t // tt),
        in_specs=[
            pl.BlockSpec((bb, tt, wb), lambda i, j: (i, j, 0)),
            pl.BlockSpec((bb, tt, wb), lambda i, j: (i, j, 1)),
            st,
            pl.BlockSpec((bb, SUBLANES, wb), lambda i, j: (i, 0, 0)),
            pl.BlockSpec((CONV_W, wb), row), vec,
            pl.BlockSpec(wa4.shape, lambda i, j: (0, 0, 0)), vec,
            pl.BlockSpec(wx4.shape, lambda i, j: (0, 0, 0)), vec, vec,
        ],
        out_specs=[pl.BlockSpec((bb, tt, wb), lambda i, j: (i, j, 0)), st],
        out_shape=[jax.ShapeDtypeStruct((b, t, wb), F32), jax.ShapeDtypeStruct((b, 1, wb), F32)],
        scratch_shapes=[pltpu.VMEM((bb, SUBLANES, wb), F32), pltpu.VMEM((bb, 1, wb), F32),
                        pltpu.VMEM((bb, tt, wb), F32), pltpu.VMEM((bb, tt, wb), F32)],
        compiler_params=_params("parallel", "arbitrary"),
        name="rglru",
    )(zb3, zb3, h0, hist0, cw, cb, wa4, ba, wx4, bx, lam)


def _s5_setup_kernel(are_ref, aim_ref, ldt_ref, bre_ref, bim_ref, cre_ref, cim_ref,
                     bd_ref, gm_ref, hm_ref, pre_ref, pim_ref):
    a_re = are_ref[...]
    a_im = aim_ref[...]
    dt = jnp.exp(ldt_ref[...])
    mag = jnp.exp(dt * a_re)
    ab_re = mag * jnp.cos(dt * a_im)
    ab_im = mag * jnp.sin(dt * a_im)
    den = a_re * a_re + a_im * a_im
    f_re = ((ab_re - 1.0) * a_re + ab_im * a_im) / den
    f_im = (ab_im * a_re - (ab_re - 1.0) * a_im) / den
    b_re = bre_ref[...]
    b_im = bim_ref[...]
    bb_re = f_re[:, None, :] * b_re - f_im[:, None, :] * b_im
    bb_im = f_re[:, None, :] * b_im + f_im[:, None, :] * b_re
    c_re = cre_ref[...]
    c_im = cim_ref[...]
    rhs = jnp.concatenate([bb_re, bb_im], axis=2)
    gb, n_c, n_p = c_re.shape
    gc = gb * n_c
    sw = gb * n_p
    ii = lambda shape, dim: lax.broadcasted_iota(jnp.int32, shape, dim)
    same = lambda shape, rdiv, ldiv: ii(shape, 0) // rdiv == ii(shape, 1) // ldiv
    tile_p = (ii((n_p, sw), 0) == ii((n_p, sw), 1) % n_p).astype(BF16)
    tile_c = (ii((n_c, gc), 0) == ii((n_c, gc), 1) % n_c).astype(BF16)
    wide = lambda x, tile: jnp.dot(x.astype(BF16), tile, preferred_element_type=F32)
    lane_grp = ii((n_p, gc), 1) // n_c
    p_re = jnp.ones_like(a_re)
    p_im = jnp.zeros_like(a_re)
    for tau in range(S5_CHUNK + 1):
        pre_ref[:, tau, :] = p_re
        pim_ref[:, tau, :] = p_im
        m_re = c_re * p_re[:, None, :] - c_im * p_im[:, None, :]
        m_im = c_re * p_im[:, None, :] + c_im * p_re[:, None, :]
        if tau < S5_CHUNK:
            s = S5_CHUNK - 1 - tau
            pb_re = p_re[:, None, :] * bb_re - p_im[:, None, :] * bb_im
            pb_im = p_re[:, None, :] * bb_im + p_im[:, None, :] * bb_re
            for ri, pb in enumerate((pb_re, pb_im)):
                blk = jnp.where(same((gc, sw), n_c, n_p), wide(pb.reshape(gc, n_p), tile_p), 0.0)
                gm_ref[0, s * gc:(s + 1) * gc, ri * sw:(ri + 1) * sw] = blk.astype(BF16)
            lhs = jnp.concatenate([m_re, -m_im], axis=2)
            k = jnp.einsum("gmk,gnk->gmn", lhs, rhs, precision=lax.Precision.HIGHEST,
                           preferred_element_type=F32)
            kw = jnp.where(same((gc, gc), n_c, n_c), wide(k.reshape(gc, n_c), tile_c), 0.0)
            bd_ref[0, tau] = kw.T.astype(BF16)
        if tau >= 1:
            for ri, m in enumerate((m_re, -m_im)):
                m2 = m.reshape(gc, n_p).astype(BF16).astype(F32)
                mt = jnp.concatenate([m2, jnp.zeros_like(m2)], axis=1).T[:n_p]
                for g in range(gb):
                    hm_ref[0, ri * sw + g * n_p:ri * sw + (g + 1) * n_p, (tau - 1) * gc:tau * gc] = (
                        jnp.where(lane_grp == g, mt, 0.0).astype(BF16))
        p_re, p_im = p_re * ab_re - p_im * ab_im, p_re * ab_im + p_im * ab_re


def _s5_setup(a_re, a_im, log_dt, b_re_t, b_im_t, c_re, c_im):
    g, p = a_re.shape
    c = c_re.shape[1]
    gb = LANES // c
    n_unit = g // gb
    nt = S5_CHUNK + 1
    lw = S5_CHUNK * LANES
    sw = gb * p
    g2 = pl.BlockSpec((gb, p), lambda i: (i, 0))
    g3 = pl.BlockSpec((gb, c, p), lambda i: (i, 0, 0))
    o3 = pl.BlockSpec((gb, nt, p), lambda i: (i, 0, 0))
    s3 = jax.ShapeDtypeStruct((g, nt, p), F32)
    return pl.pallas_call(
        _s5_setup_kernel,
        grid=(n_unit,),
        in_specs=[g2, g2, pl.BlockSpec((gb, 1), lambda i: (i, 0)), g3, g3, g3, g3],
        out_specs=[pl.BlockSpec((1, S5_CHUNK, LANES, LANES), lambda i: (i, 0, 0, 0)),
                   pl.BlockSpec((1, lw, 2 * sw), lambda i: (i, 0, 0)),
                   pl.BlockSpec((1, 2 * sw, lw), lambda i: (i, 0, 0)), o3, o3],
        out_shape=[jax.ShapeDtypeStruct((n_unit, S5_CHUNK, LANES, LANES), BF16),
                   jax.ShapeDtypeStruct((n_unit, lw, 2 * sw), BF16),
                   jax.ShapeDtypeStruct((n_unit, 2 * sw, lw), BF16), s3, s3],
        compiler_params=_params("parallel"),
        name="s5_setup",
    )(a_re, a_im, log_dt, b_re_t, b_im_t, c_re, c_im)


def _s5_kernel(u_ref, h0r_ref, h0i_ref, bd_ref, gm_ref, hm_ref, ar_ref, ai_ref,
               y_ref, her_ref, hei_ref, kt_scr, hs_scr, *, chunk, n_chunk):
    sw = ar_ref.shape[-1]
    rows = u_ref.shape[0] // chunk

    for s in range(chunk):
        for t in range(s, chunk):
            kt_scr[s * LANES:(s + 1) * LANES, t * LANES:(t + 1) * LANES] = bd_ref[0, t - s]
        if s % 2:
            kt_scr[s * LANES:(s + 1) * LANES, (s - 1) * LANES:s * LANES] = jnp.zeros((LANES, LANES), BF16)

    ucat = jnp.concatenate(
        [u_ref[pl.ds(s, rows, stride=chunk), :] for s in range(chunk)], axis=1).astype(BF16)
    gu = jnp.dot(ucat, gm_ref[0], preferred_element_type=F32)
    a_r = ar_ref[0]
    a_i = ai_ref[0]

    def advance(h_r, h_i, g):
        return a_r * h_r - a_i * h_i + g[:, :sw], a_r * h_i + a_i * h_r + g[:, sw:]

    if n_chunk == 1:
        h_r = h0r_ref[...]
        h_i = h0i_ref[...]
        hs = jnp.concatenate([h_r, h_i], axis=1)
        e_r, e_i = advance(h_r, h_i, gu)
        her_ref[...] = e_r
        hei_ref[...] = e_i
    else:
        hs_scr[...] = gu
        group = SUBLANES if n_chunk % SUBLANES == 0 else n_chunk
        sub = lax.broadcasted_iota(jnp.int32, (group, 2 * sw), 0)
        n_seq = rows // n_chunk

        def tile(it, hs_all):
            out = []
            for b in range(n_seq):
                h_r, h_i = hs_all[b]
                off = pl.multiple_of(b * n_chunk + it * group, group)
                g8 = hs_scr[pl.ds(off, group), :]
                hs8 = jnp.zeros((group, 2 * sw), F32)
                for i in range(group):
                    hs8 = jnp.where(sub == i, jnp.concatenate([h_r, h_i], axis=1), hs8)
                    h_r, h_i = advance(h_r, h_i, g8[i:i + 1, :])
                hs_scr[pl.ds(off, group), :] = hs8
                out.append((h_r, h_i))
            return tuple(out)

        ends = lax.fori_loop(0, n_chunk // group, tile,
                             tuple((h0r_ref[b], h0i_ref[b]) for b in range(n_seq)))
        for b in range(n_seq):
            her_ref[b] = ends[b][0]
            hei_ref[b] = ends[b][1]
        hs = hs_scr[...]
    hsb = hs.astype(BF16)
    for p in range(chunk // 2):
        k_hi = (2 * p + 2) * LANES
        cols = slice(2 * p * LANES, k_hi)
        y = (jnp.dot(ucat[:, :k_hi], kt_scr[:k_hi, cols], preferred_element_type=F32)
             + jnp.dot(hsb, hm_ref[0, :, cols], preferred_element_type=F32))
        for q in range(2):
            y_ref[pl.ds(2 * p + q, rows, stride=chunk), :] = y[:, q * LANES:(q + 1) * LANES]


def _s5_conv(u, h0r, h0i, bd, gm, hm, ar, ai, b, t, chunk):
    n, d = u.shape
    n_unit = d // LANES
    sw = ar.shape[-1]
    n_chunk = t // chunk
    lw = chunk * LANES
    if n_chunk == 1:
        hspec = pl.BlockSpec((b, sw), lambda q: (0, q))
        hshape = jax.ShapeDtypeStruct((b, n_unit * sw), F32)
    else:
        h0r, h0i = h0r[:, None, :], h0i[:, None, :]
        hspec = pl.BlockSpec((b, 1, sw), lambda q: (0, 0, q))
        hshape = jax.ShapeDtypeStruct((b, 1, n_unit * sw), F32)
    rows = b * n_chunk
    tok = pl.BlockSpec((n, LANES), lambda q: (0, q))
    y, he_r, he_i = pl.pallas_call(
        functools.partial(_s5_kernel, chunk=chunk, n_chunk=n_chunk),
        grid=(n_unit,),
        in_specs=[
            tok, hspec, hspec,
            pl.BlockSpec((1, chunk, LANES, LANES), lambda q: (q, 0, 0, 0)),
            pl.BlockSpec((1, lw, 2 * sw), lambda q: (q, S5_CHUNK // chunk - 1, 0)),
            pl.BlockSpec((1, 2 * sw, lw), lambda q: (q, 0, 0)),
            pl.BlockSpec((1, 1, sw), lambda q: (q, 0, 0)),
            pl.BlockSpec((1, 1, sw), lambda q: (q, 0, 0)),
        ],
        out_specs=[tok, hspec, hspec],
        out_shape=[jax.ShapeDtypeStruct((n, d), F32), hshape, hshape],
        scratch_shapes=[pltpu.VMEM((lw, lw), BF16), pltpu.VMEM((rows, 2 * sw), F32)],
        compiler_params=_params("parallel"),
        name="s5_conv",
    )(u, h0r, h0i, bd, gm, hm, ar, ai)
    return y, he_r.reshape(b, -1), he_i.reshape(b, -1)


def _s5_out_kernel(x_ref, u_ref, yc_ref, d_ref, w_ref, b_ref, o_ref):
    z = jax.nn.gelu(yc_ref[...] + d_ref[...] * u_ref[...])
    o_ref[...] = x_ref[...] + z * jax.nn.sigmoid(_bdot(z, w_ref[...]) + b_ref[...])


def _s5_out(x, u, yc, d, w, b):
    n, dm = x.shape
    tm = _row_tile(n, 256)
    row = lambda i: (0, 0)
    tile = pl.BlockSpec((tm, dm), lambda i: (i, 0))
    vec = pl.BlockSpec((1, dm), row)
    return pl.pallas_call(
        _s5_out_kernel,
        grid=(n // tm,),
        in_specs=[tile, tile, tile, vec, pl.BlockSpec((dm, dm), row), vec],
        out_specs=tile,
        out_shape=jax.ShapeDtypeStruct((n, dm), F32),
        compiler_params=_params("parallel"),
        name="s5_out",
    )(x, u, yc, d, w, b)


def _block_diag_tiles(w, per_tile):
    n_blk, h, _ = w.shape
    n_tile = n_blk // per_tile
    w = w.reshape(n_tile, per_tile, h, h)
    eye = jnp.eye(per_tile, dtype=w.dtype)
    out = jnp.einsum("tphk,pq->tphqk", w, eye)
    return out.reshape(n_tile, per_tile * h, per_tile * h)


def _head_sum_mats(wa):
    onehot = (jnp.arange(wa)[:, None] // HEAD == jnp.arange(LANES)[None, :])
    return onehot.astype(BF16), onehot.T.astype(BF16)


def _wkv_on_lanes(vecs, v, s0):
    b, t, wa = v.shape
    h = wa // HEAD
    t8 = t // SUBLANES
    tt = min(t, 64)
    if b * h <= LANES:
        rep = LANES // (b * h)
        ip = HEAD // rep
        tp = min(t, LANES)
        keys = [_to_lanes(x, rep, False, tp)[None] for x in vecs]
        v2 = _to_lanes(v, rep, True, tp).reshape(1, t, ip, LANES)
        s2 = s0.reshape(b, h, rep, ip, HEAD).transpose(4, 3, 2, 0, 1).reshape(1, HEAD, ip, LANES)
        y2, s2 = _wkv(keys, v2, s2, tt)
        y = _from_lanes(y2.reshape(t * ip, LANES), b, wa, rep, tp)
        s = s2.reshape(HEAD, ip, rep, b, h).transpose(3, 4, 2, 1, 0).reshape(b, h, HEAD, HEAD)
    else:
        nb = b // LANES
        lay = lambda x: x.reshape(nb, LANES, t8, SUBLANES, h, HEAD).transpose(4, 0, 5, 2, 3, 1).reshape(
            h * nb, HEAD, t8, SUBLANES, LANES)
        v2 = v.reshape(nb, LANES, t, h, HEAD).transpose(3, 0, 2, 4, 1).reshape(h * nb, t, HEAD, LANES)
        s2 = s0.reshape(nb, LANES, h, HEAD, HEAD).transpose(2, 0, 4, 3, 1).reshape(h * nb, HEAD, HEAD, LANES)
        y2, s2 = _wkv([lay(x) for x in vecs], v2, s2, tt)
        y = y2.reshape(h, nb, t, HEAD, LANES).transpose(1, 4, 2, 0, 3).reshape(b, t, wa)
        s = s2.reshape(h, nb, HEAD, HEAD, LANES).transpose(1, 4, 0, 3, 2).reshape(b, h, HEAD, HEAD)
    return y, s


def _seq_blocks(b, t):
    if t >= 256:
        return 1, 256
    return min(b, max(1, 256 // t)), t


def _prepare(w):
    depth = w["norm_ffn1"].shape[0]
    wa = w["w0_a"].shape[1]
    wb = w["lam_b"].shape[1]
    cols_a = w["mu_a"].shape[1]
    ca = ((cols_a + 511) // 512) * 512
    row = lambda v: v.reshape(1, -1)
    bf = lambda v: v.astype(BF16)
    pad_a = lambda v: jnp.pad(v, [(0, 0)] * (v.ndim - 1) + [(0, ca - cols_a)])
    layers = []
    for l in range(depth):
        j = l // 2
        q = dict(ple_gate=bf(w["ple_gate"][l]), ple_proj=bf(w["ple_proj"][l]))
        if l % 2 == 0:
            w_in = w["w_in_ab"][j]
            n_q = wb // (2 * LANES)
            w_out = bf(w["w_out_ab"][j])
            q.update(
                w_in_a=bf(pad_a(w_in[:, :cols_a])), w_in_b=bf(w_in[:, cols_a:]), mu=row(pad_a(w["mu_a"][j])),
                w2=bf(jnp.zeros((LANES, wa), F32).at[:LORA_W].set(w["w2_a"][j])),
                a2=bf(jnp.zeros((LANES, wa), F32).at[LORA_W:LORA_W + LORA_A].set(w["a2_a"][j])),
                g2=bf(jnp.zeros((2 * LANES, wa), F32).at[:LORA_G].set(w["g2_a"][j])),
                wa4=bf(_block_diag_tiles(w["wa_b"][j], (wb // HEAD) // n_q)),
                wx4=bf(_block_diag_tiles(w["wx_b"][j], (wb // HEAD) // n_q)),
                w_out_a=w_out[:wa], w_out_b=w_out[wa:])
        else:
            q.update(
                tabs=_s5_setup(w["a_re_c"][j], w["a_im_c"][j], w["log_dt_c"][j][:, None],
                               w["b_re_c"][j].transpose(0, 2, 1), w["b_im_c"][j].transpose(0, 2, 1),
                               w["c_re_c"][j], w["c_im_c"][j]),
                w_glu=bf(w["w_glu_c"][j]))
        layers.append(q)
    return layers


def _trunk(x, p, st_wkv, st_shift, st_h, st_conv, st_cre, st_cim, w, wp):
    b, t, d = x.shape
    n = b * t
    depth = w["norm_ffn1"].shape[0]
    wa = w["w0_a"].shape[1]
    wb = w["lam_b"].shape[1]
    cols_a = w["mu_a"].shape[1]
    ca = wp[0]["mu"].shape[1]
    n_head = wa // HEAD
    n_grp = w["a_re_c"].shape[1]
    chunk = min(S5_CHUNK, t)
    ered, eexp = _head_sum_mats(wa)
    row = lambda v: v.reshape(1, -1)

    x = x.reshape(n, d)
    new = {k: [] for k in ("wkv", "shift", "h", "conv", "cre", "cim")}
    for l in range(depth):
        j = l // 2
        q = wp[l]
        x = _ffn(x, row(w["norm_ffn1"][l]), w["ffn1_wg"], w["ffn1_wu"], w["ffn1_wd"], l)
        if l % 2 == 0:
            za, zb = _inproj(x, row(w["norm_mix"][l]), q["w_in_a"], q["w_in_b"])
            z3 = za.reshape(b, t, ca)
            zb3 = zb.reshape(b, t, 2 * wb)
            shift = jnp.pad(st_shift[j], ((0, 0), (0, ca - cols_a)))[:, None, :]
            bb, tt = _seq_blocks(b, t)
            r, dec, k2, v, nkk, kka, g = _rwkv_prep(
                z3, shift, q["mu"], row(w["w0_a"][j]), q["w2"], row(w["a0_a"][j]), q["a2"], q["g2"],
                row(w["kk_a"][j]), row(w["ka_a"][j]), ered, eexp, wa, ca, bb, tt)
            y, s_new = _wkv_on_lanes([nkk, dec, kka, k2, r], v, st_wkv[j])
            hist0 = jnp.pad(st_conv[j], ((0, 0), (SUBLANES - (CONV_W - 1), 0), (0, 0)))
            yb, h_new = _rglru(
                zb3, st_h[j][:, None, :], hist0, w["conv_w_b"][j], row(w["conv_b_b"][j]), q["wa4"],
                row(w["ba_b"][j]), q["wx4"], row(w["bx_b"][j]), row(w["lam_b"][j]), bb, tt)
            f2 = lambda a: a.reshape(n, -1)
            x = _mix_out(x, f2(y), f2(r), f2(k2), f2(v), f2(g), f2(yb), row(w["rk_a"][j]),
                         row(w["lnx_g"][j]), row(w["lnx_b"][j]), ered, eexp, q["w_out_a"], q["w_out_b"])
            new["wkv"].append(s_new)
            new["shift"].append(z3[:, t - 1, :cols_a])
            new["h"].append(h_new[:, 0, :])
            conv_all = jnp.concatenate([st_conv[j], zb3[:, max(t - (CONV_W - 1), 0):, :wb]], axis=1)
            new["conv"].append(conv_all[:, conv_all.shape[1] - (CONV_W - 1):])
        else:
            u = _rmsnorm(x, row(w["norm_mix"][l]))
            bd, gm, hm, p_re, p_im = q["tabs"]
            ar = p_re[:, chunk].reshape(bd.shape[0], 1, -1)
            ai = p_im[:, chunk].reshape(bd.shape[0], 1, -1)
            yc, he_r, he_i = _s5_conv(u, st_cre[j].reshape(b, -1), st_cim[j].reshape(b, -1),
                                      bd, gm, hm, ar, ai, b, t, chunk)
            x = _s5_out(x, u, yc, row(w["d_c"][j]), q["w_glu"], row(w["b_glu_c"][j]))
            new["cre"].append(he_r.reshape(b, n_grp, P_C))
            new["cim"].append(he_i.reshape(b, n_grp, P_C))
        x = _ffn(x, row(w["norm_ffn2"][l]), w["ffn2_wg"], w["ffn2_wu"], w["ffn2_wd"], l)
        x = _ple(x, row(w["norm_ple"][l]), q["ple_gate"], p[l].reshape(n, -1), q["ple_proj"],
                 row(w["final_norm"]) if l == depth - 1 else None)
    y = x.reshape(b, t, d)
    stk = lambda name, ref: jnp.stack(new[name]).astype(ref.dtype)
    return y, (stk("wkv", st_wkv), stk("shift", st_shift), stk("h", st_h),
               stk("conv", st_conv), stk("cre", st_cre), stk("cim", st_cim))


def kernel(x_prompt, x_sample, state_a_wkv, state_a_shift, state_b_h, state_b_conv, state_c_re, state_c_im, p_prompt, p_sample, norm_ffn1, ffn1_wg, ffn1_wu, ffn1_wd, norm_mix, norm_ffn2, ffn2_wg, ffn2_wu, ffn2_wd, norm_ple, ple_gate, ple_proj, w_in_ab, mu_a, w0_a, w2_a, a0_a, a2_a, g2_a, kk_a, ka_a, rk_a, lnx_g, lnx_b, conv_w_b, conv_b_b, wa_b, ba_b, wx_b, bx_b, lam_b, w_out_ab, a_re_c, a_im_c, log_dt_c, b_re_c, b_im_c, c_re_c, c_im_c, d_c, w_glu_c, b_glu_c, final_norm):
    w = dict(norm_ffn1=norm_ffn1, ffn1_wg=ffn1_wg, ffn1_wu=ffn1_wu, ffn1_wd=ffn1_wd,
             norm_mix=norm_mix, norm_ffn2=norm_ffn2, ffn2_wg=ffn2_wg, ffn2_wu=ffn2_wu,
             ffn2_wd=ffn2_wd, norm_ple=norm_ple, ple_gate=ple_gate, ple_proj=ple_proj,
             w_in_ab=w_in_ab, mu_a=mu_a, w0_a=w0_a, w2_a=w2_a, a0_a=a0_a, a2_a=a2_a, g2_a=g2_a,
             kk_a=kk_a, ka_a=ka_a, rk_a=rk_a.reshape(rk_a.shape[0], -1), lnx_g=lnx_g, lnx_b=lnx_b,
             conv_w_b=conv_w_b, conv_b_b=conv_b_b, wa_b=wa_b, ba_b=ba_b, wx_b=wx_b, bx_b=bx_b,
             lam_b=lam_b, w_out_ab=w_out_ab,
             a_re_c=a_re_c, a_im_c=a_im_c, log_dt_c=log_dt_c, b_re_c=b_re_c, b_im_c=b_im_c,
             c_re_c=c_re_c, c_im_c=c_im_c, d_c=d_c, w_glu_c=w_glu_c, b_glu_c=b_glu_c,
             final_norm=final_norm)
    wp = _prepare(w)
    bp = x_prompt.shape[0]
    zeros = lambda s: jnp.zeros((s.shape[0], bp) + s.shape[2:], s.dtype)
    y_prompt, prompt_state = _trunk(
        x_prompt, p_prompt, zeros(state_a_wkv), zeros(state_a_shift), zeros(state_b_h),
        zeros(state_b_conv), zeros(state_c_re), zeros(state_c_im), w, wp)
    y_sample, sample_state = _trunk(
        x_sample, p_sample, state_a_wkv, state_a_shift, state_b_h, state_b_conv,
        state_c_re, state_c_im, w, wp)
    return (y_prompt, y_sample) + tuple(prompt_state) + tuple(sample_state)
```
